```python
import jax, jax.numpy as jnp
from jax import lax
import numpy as np

D_MODEL = 4096
BATCH = 16
SEQ = 256
DEPTH = 2
DEC_BATCH = 2
DEC_SEQ = 2048
PAST_LEN = 512

GRID_W = 64
NA_HEADS = 8
NA_WIDTH = D_MODEL // 4
NA_HEAD_DIM = NA_WIDTH // NA_HEADS
NA_WIN_ROWS = 8
NA_WIN_COLS = 16
POOL_WINDOWS = (2, 4, 8, 16)
POOL_WIDTH = D_MODEL // 4
POOL_GROUP = POOL_WIDTH // 4
ML_HEADS = 8
ML_WIDTH = D_MODEL // 2
ML_HEAD_DIM = ML_WIDTH // ML_HEADS
ML_CHUNK = 64
D_FF = 256 * ((8 * D_MODEL // 3 + 255) // 256)
N_MOD = 9
MACARON_W = 0.5
Q_BLOCK = 128
EPS = 1e-6
MASK_VALUE = -1e30
PROJ_SIZES = (NA_WIDTH,) * 3 + (POOL_WIDTH,) + (ML_WIDTH,) * 4 + (4 * ML_HEADS,) + (D_MODEL,) * 3
IN_COLS = sum(PROJ_SIZES)
MIX_WIDTH = NA_WIDTH + POOL_WIDTH + ML_WIDTH

kernel_name = 'hybrid_na_pool_mlstm_flow_step'

F32 = jnp.float32


def _rmsnorm(x, g):
    xf = x.astype(F32)
    y = xf * lax.rsqrt(jnp.mean(xf * xf, axis=-1, keepdims=True) + EPS)
    return (y * g.astype(F32)).astype(x.dtype)


def _modulation(cond, w_ada, b_ada):
    m = jax.nn.silu(cond) @ w_ada + b_ada
    return m.reshape(cond.shape[0], N_MOD, D_MODEL)


def _modulate(x, g, shift, scale):
    return _rmsnorm(x, g) * (1 + scale[:, None]) + shift[:, None]


def _ffn(x, mod, j, g, w_in, w_out):
    h = _modulate(x, g, mod[:, 3 * j], mod[:, 3 * j + 1])
    a, b = jnp.split(h @ w_in, 2, axis=-1)
    return x + MACARON_W * mod[:, 3 * j + 2][:, None] * ((jax.nn.silu(a) * b) @ w_out)


def _split_proj(p):
    offs = np.cumsum(PROJ_SIZES)[:-1].tolist()
    return jnp.split(p, offs, axis=-1)


def _context_attention(q, k, v):
    B, T, H, Dh = q.shape
    nb = T // Q_BLOCK
    qb = jnp.moveaxis(q.reshape(B, nb, Q_BLOCK, H, Dh), 1, 0)
    scale = Dh ** -0.5

    def block(qi):
        s = jnp.einsum('bqhd,bkhd->bhqk', qi, k).astype(F32) * scale
        p = jax.nn.softmax(s, axis=-1).astype(v.dtype)
        return jnp.einsum('bhqk,bkhd->bqhd', p, v)

    out = lax.map(block, qb)
    return jnp.moveaxis(out, 0, 1).reshape(B, T, H * Dh)


def _neighbourhood_attention(q, k, v, ctx_k, ctx_v, rel_bias):
    B, T, H, Dh = q.shape
    rows = T // GRID_W
    kr = min(NA_WIN_ROWS, rows)
    kc = NA_WIN_COLS
    r = jnp.arange(rows)
    cq = jnp.arange(GRID_W)
    row_idx = jnp.clip(r - kr // 2, 0, rows - kr)[:, None] + jnp.arange(kr)[None, :]
    col_start = jnp.clip(cq - kc // 2, 0, GRID_W - kc)
    col_mask = (cq[None, :] >= col_start[:, None]) & (cq[None, :] < col_start[:, None] + kc)
    dr = row_idx - r[:, None] + (NA_WIN_ROWS - 1)
    dc = jnp.clip(cq[None, :] - cq[:, None], -(kc - 1), kc - 1) + (NA_WIN_COLS - 1)
    bias = rel_bias[:, dr[:, None, :, None], dc[None, :, None, :]]
    bias = jnp.transpose(bias, (1, 2, 0, 3, 4)).astype(F32)
    qg = q.reshape(B, rows, GRID_W, H, Dh)
    kg = k.reshape(B, rows, GRID_W, H, Dh)[:, row_idx]
    vg = v.reshape(B, rows, GRID_W, H, Dh)[:, row_idx]
    scale = Dh ** -0.5
    s_loc = jnp.einsum('brwhd,brixhd->brwhix', qg, kg).astype(F32) * scale + bias[None]
    s_loc = jnp.where(col_mask[None, None, :, None, None, :], s_loc, MASK_VALUE)
    s_ctx = jnp.einsum('brwhd,bphd->brwhp', qg, ctx_k).astype(F32) * scale
    n_loc = kr * GRID_W
    s = jnp.concatenate([s_loc.reshape(B, rows, GRID_W, H, n_loc), s_ctx], axis=-1)
    p = jax.nn.softmax(s, axis=-1).astype(v.dtype)
    p_loc = p[..., :n_loc].reshape(B, rows, GRID_W, H, kr, GRID_W)
    p_ctx = p[..., n_loc:]
    out = (jnp.einsum('brwhix,brixhd->brwhd', p_loc, vg)
           + jnp.einsum('brwhp,bphd->brwhd', p_ctx, ctx_v))
    return out.reshape(B, T, H * Dh)


def _multiscale_pool(u, pool_w, pool_scale):
    B, T, _ = u.shape
    uf = u.astype(F32).reshape(B, T, 4, POOL_GROUP)
    csum = jnp.concatenate([jnp.zeros((B, 1, 4, POOL_GROUP), F32), jnp.cumsum(uf, axis=1)], axis=1)
    pos = jnp.arange(T)
    means = []
    for g, w in enumerate(POOL_WINDOWS):
        lo = jnp.clip(pos - w // 2, 0, T)
        hi = jnp.clip(pos + w - w // 2, 0, T)
        cnt = (hi - lo).astype(F32)[None, :, None]
        cg = csum[:, :, g]
        means.append((cg[:, hi] - cg[:, lo]) / cnt)
    pooled = jnp.stack(means, axis=2) - uf
    y = jnp.einsum('btgc,gcd->btgd', pooled.astype(u.dtype), pool_w)
    return y.reshape(B, T, POOL_WIDTH) * pool_scale


def _mlstm_scan(q, k, v, log_i, log_f, state):
    B, T, H, Dh = q.shape
    L = ML_CHUNK
    nc = T // L

    def to_chunks(a):
        a = jnp.moveaxis(a.reshape((B, nc, L) + a.shape[2:]), 1, 0)
        return jnp.swapaxes(a, 2, 3)

    causal = jnp.tril(jnp.ones((L, L), dtype=bool))

    def step(carry, inp):
        C, n, m = carry
        qb, kb, vb, ib, fb = inp
        b = jnp.cumsum(fb, axis=-1)
        dmat = jnp.where(causal, b[..., :, None] - b[..., None, :] + ib[..., None, :], -jnp.inf)
        inter = b + m[..., None]
        m_t = jnp.maximum(inter, jnp.max(dmat, axis=-1))
        w_inter = jnp.exp(inter - m_t)
        qk = jnp.einsum('bhtd,bhsd->bhts', qb, kb) * jnp.exp(dmat - m_t[..., None])
        num = (jnp.einsum('bhts,bhse->bhte', qk, vb)
               + w_inter[..., None] * jnp.einsum('bhtd,bhde->bhte', qb, C))
        den = jnp.sum(qk, axis=-1) + w_inter * jnp.einsum('bhtd,bhd->bht', qb, n)
        h = num / jnp.maximum(jnp.abs(den), jnp.exp(-m_t))[..., None]
        g = b[..., -1:] - b + ib
        m_new = jnp.maximum(b[..., -1] + m, jnp.max(g, axis=-1))
        ws = jnp.exp(g - m_new[..., None])
        decay = jnp.exp(b[..., -1] + m - m_new)
        C_new = decay[..., None, None] * C + jnp.einsum('bhsd,bhse->bhde', kb * ws[..., None], vb)
        n_new = decay[..., None] * n + jnp.einsum('bhs,bhsd->bhd', ws, kb)
        return (C_new, n_new, m_new), h

    xs = (to_chunks(q), to_chunks(k), to_chunks(v), to_chunks(log_i), to_chunks(log_f))
    final, hs = lax.scan(step, state, xs)
    h = jnp.moveaxis(jnp.swapaxes(hs, 2, 3), 0, 1).reshape(B, T, H, Dh)
    return h, final


def _mlstm_branch(q, k, v, o_pre, gate_pre, gate_bias, norm_g, init_f, init_b):
    B, T, _ = q.shape
    shp = (B, T, ML_HEADS, ML_HEAD_DIM)
    q = q.reshape(shp).astype(F32)
    k = k.reshape(shp).astype(F32) * ML_HEAD_DIM ** -0.5
    v = v.reshape(shp).astype(F32)
    gp = gate_pre.astype(F32) + gate_bias.astype(F32)
    i_f, f_f, i_b, f_b = jnp.split(gp, 4, axis=-1)
    h_f, st_f = _mlstm_scan(q, k, v, i_f, jax.nn.log_sigmoid(f_f), init_f)
    h_b, st_b = _mlstm_scan(q[:, ::-1], k[:, ::-1], v[:, ::-1], i_b[:, ::-1],
                            jax.nn.log_sigmoid(f_b[:, ::-1]), init_b)
    h = h_f + h_b[:, ::-1]
    h = h * lax.rsqrt(jnp.mean(h * h, axis=-1, keepdims=True) + EPS)
    h = h.reshape(B, T, ML_WIDTH) * norm_g.astype(F32)
    out = (jax.nn.sigmoid(o_pre.astype(F32)) * h).astype(o_pre.dtype)
    return out, st_f, st_b


def _mixer(x, mod, g, w_in, rel_bias, pool_w, pool_scale, ml_gate_bias, ml_norm_g, w_branch, w_out,
           ctx_k, ctx_v, init_f, init_b):
    B, T, _ = x.shape
    h = _modulate(x, g, mod[:, 3], mod[:, 4])
    (na_q, na_k, na_v, pool_u, ml_q, ml_k, ml_v, ml_o, ml_g,
     g_na, g_pool, g_ml) = _split_proj(h @ w_in)
    na_shape = (B, T, NA_HEADS, NA_HEAD_DIM)
    na_q = na_q.reshape(na_shape)
    na_k = na_k.reshape(na_shape)
    na_v = na_v.reshape(na_shape)
    if ctx_k is None:
        na_out = _context_attention(na_q, na_k, na_v)
    else:
        na_out = _neighbourhood_attention(na_q, na_k, na_v, ctx_k, ctx_v, rel_bias)
    pool_out = _multiscale_pool(pool_u, pool_w, pool_scale)
    if init_f is None:
        zero = (jnp.zeros((B, ML_HEADS, ML_HEAD_DIM, ML_HEAD_DIM), F32),
                jnp.zeros((B, ML_HEADS, ML_HEAD_DIM), F32),
                jnp.zeros((B, ML_HEADS), F32))
        init_f = zero
        init_b = zero
    ml_out, st_f, st_b = _mlstm_branch(ml_q, ml_k, ml_v, ml_o, ml_g, ml_gate_bias, ml_norm_g, init_f, init_b)
    wb_na, wb_pool, wb_ml = jnp.split(w_branch, [NA_WIDTH, NA_WIDTH + POOL_WIDTH], axis=0)
    merged = (jax.nn.sigmoid(g_na) * (na_out @ wb_na)
              + jax.nn.sigmoid(g_pool) * (pool_out @ wb_pool)
              + jax.nn.sigmoid(g_ml) * (ml_out @ wb_ml))
    x = x + mod[:, 5][:, None] * (merged @ w_out)
    return x, na_k, na_v, st_f, st_b


def setup_inputs(seed: int = 0) -> dict:
    key = jax.random.key(seed)
    ks = jax.random.split(key, 32)

    def nrm(k, shape, s):
        return jax.random.normal(k, shape, F32) * s

    return {
        'x_prompt': nrm(ks[0], (BATCH, SEQ, D_MODEL), 1.0),
        'x_sample': nrm(ks[1], (DEC_BATCH, DEC_SEQ, D_MODEL), 1.0),
        'c': nrm(ks[2], (DEC_BATCH, D_MODEL), 1.0),
        'cache_na_k': nrm(ks[3], (DEC_BATCH, DEPTH, PAST_LEN, NA_HEADS, NA_HEAD_DIM), 1.0),
        'cache_na_v': nrm(ks[4], (DEC_BATCH, DEPTH, PAST_LEN, NA_HEADS, NA_HEAD_DIM), 1.0),
        'state_mlstm_C': nrm(ks[5], (DEC_BATCH, DEPTH, 2, ML_HEADS, ML_HEAD_DIM, ML_HEAD_DIM), 0.05),
        'state_mlstm_n': nrm(ks[6], (DEC_BATCH, DEPTH, 2, ML_HEADS, ML_HEAD_DIM), 0.1),
        'state_mlstm_m': nrm(ks[7], (DEC_BATCH, DEPTH, 2, ML_HEADS), 1.0),
        'c_ctx': nrm(ks[8], (D_MODEL,), 1.0),
        'w_ada': nrm(ks[9], (DEPTH, D_MODEL, N_MOD * D_MODEL), 0.5 * D_MODEL ** -0.5),
        'b_ada': nrm(ks[10], (DEPTH, N_MOD * D_MODEL), 0.02),
        'norm_g': 1.0 + nrm(ks[11], (DEPTH, 3, D_MODEL), 0.02),
        'ffn_w_in': nrm(ks[12], (DEPTH, 2, D_MODEL, 2 * D_FF), D_MODEL ** -0.5),
        'ffn_w_out': nrm(ks[13], (DEPTH, 2, D_FF, D_MODEL), D_FF ** -0.5),
        'w_in': nrm(ks[14], (DEPTH, D_MODEL, IN_COLS), D_MODEL ** -0.5),
        'na_rel_bias': nrm(ks[15], (DEPTH, NA_HEADS, 2 * NA_WIN_ROWS - 1, 2 * NA_WIN_COLS - 1), 0.1),
        'pool_w': nrm(ks[16], (DEPTH, 4, POOL_GROUP, POOL_GROUP), POOL_GROUP ** -0.5),
        'pool_scale': 1.0 + nrm(ks[17], (DEPTH, POOL_WIDTH), 0.02),
        'ml_gate_bias': jnp.concatenate([
            nrm(ks[18], (DEPTH, ML_HEADS), 0.1),
            3.0 + nrm(ks[19], (DEPTH, ML_HEADS), 0.5),
            nrm(ks[20], (DEPTH, ML_HEADS), 0.1),
            3.0 + nrm(ks[21], (DEPTH, ML_HEADS), 0.5)], axis=-1),
        'ml_norm_g': 1.0 + nrm(ks[22], (DEPTH, ML_WIDTH), 0.02),
        'w_branch': jnp.concatenate([
            nrm(ks[23], (DEPTH, NA_WIDTH, D_MODEL), NA_WIDTH ** -0.5),
            nrm(ks[24], (DEPTH, POOL_WIDTH, D_MODEL), POOL_WIDTH ** -0.5),
            nrm(ks[25], (DEPTH, ML_WIDTH, D_MODEL), ML_WIDTH ** -0.5)], axis=1),
        'w_out': nrm(ks[26], (DEPTH, D_MODEL, D_MODEL), D_MODEL ** -0.5),
        'final_norm_g': 1.0 + nrm(ks[27], (D_MODEL,), 0.02),
    }


def reference(x_prompt, x_sample, c, cache_na_k, cache_na_v, state_mlstm_C, state_mlstm_n, state_mlstm_m,
              c_ctx, w_ada, b_ada, norm_g, ffn_w_in, ffn_w_out, w_in, na_rel_bias, pool_w, pool_scale,
              ml_gate_bias, ml_norm_g, w_branch, w_out, final_norm_g):
    xp = x_prompt
    xs = x_sample
    ks_out, vs_out, Cs_out, ns_out, ms_out = [], [], [], [], []
    for l in range(DEPTH):
        mod_p = _modulation(c_ctx[None], w_ada[l], b_ada[l])
        mod_s = _modulation(c, w_ada[l], b_ada[l])
        xp = _ffn(xp, mod_p, 0, norm_g[l, 0], ffn_w_in[l, 0], ffn_w_out[l, 0])
        xp, k_l, v_l, st_f, st_b = _mixer(xp, mod_p, norm_g[l, 1], w_in[l], na_rel_bias[l], pool_w[l],
                                          pool_scale[l], ml_gate_bias[l], ml_norm_g[l], w_branch[l], w_out[l],
                                          None, None, None, None)
        xp = _ffn(xp, mod_p, 2, norm_g[l, 2], ffn_w_in[l, 1], ffn_w_out[l, 1])
        ks_out.append(k_l)
        vs_out.append(v_l)
        Cs_out.append(jnp.stack([st_f[0], st_b[0]], axis=1))
        ns_out.append(jnp.stack([st_f[1], st_b[1]], axis=1))
        ms_out.append(jnp.stack([st_f[2], st_b[2]], axis=1))
        xs = _ffn(xs, mod_s, 0, norm_g[l, 0], ffn_w_in[l, 0], ffn_w_out[l, 0])
        init_f = (state_mlstm_C[:, l, 0].astype(F32), state_mlstm_n[:, l, 0].astype(F32),
                  state_mlstm_m[:, l, 0].astype(F32))
        init_b = (state_mlstm_C[:, l, 1].astype(F32), state_mlstm_n[:, l, 1].astype(F32),
                  state_mlstm_m[:, l, 1].astype(F32))
        xs, _, _, _, _ = _mixer(xs, mod_s, norm_g[l, 1], w_in[l], na_rel_bias[l], pool_w[l], pool_scale[l],
                                ml_gate_bias[l], ml_norm_g[l], w_branch[l], w_out[l],
                                cache_na_k[:, l], cache_na_v[:, l], init_f, init_b)
        xs = _ffn(xs, mod_s, 2, norm_g[l, 2], ffn_w_in[l, 1], ffn_w_out[l, 1])
    y_prompt = _rmsnorm(xp, final_norm_g)
    y_sample = _rmsnorm(xs, final_norm_g)
    new_na_k = jnp.stack(ks_out, axis=1)
    new_na_v = jnp.stack(vs_out, axis=1)
    new_mlstm_C = jnp.stack(Cs_out, axis=1)
    new_mlstm_n = jnp.stack(ns_out, axis=1)
    new_mlstm_m = jnp.stack(ms_out, axis=1)
    return (y_prompt, y_sample, new_na_k, new_na_v, new_mlstm_C, new_mlstm_n, new_mlstm_m)
```

```python
import functools

import jax
import jax.numpy as jnp
import numpy as np
from jax import lax
from jax.experimental import pallas as pl
from jax.experimental.pallas import tpu as pltpu

F32 = jnp.float32
BF16 = jnp.bfloat16

D_MODEL = 4096
DEPTH = 2
GRID_W = 64
NA_HEADS = 8
NA_WIDTH = D_MODEL // 4
NA_HEAD_DIM = NA_WIDTH // NA_HEADS
NA_WIN_ROWS = 8
NA_WIN_COLS = 16
POOL_WINDOWS = (2, 4, 8, 16)
POOL_WIDTH = D_MODEL // 4
POOL_GROUP = POOL_WIDTH // 4
ML_HEADS = 8
ML_WIDTH = D_MODEL // 2
ML_HEAD_DIM = ML_WIDTH // ML_HEADS
ML_CHUNK = 64
D_FF = 256 * ((8 * D_MODEL // 3 + 255) // 256)
N_MOD = 9
MACARON_W = 0.5
EPS = 1e-6
MASK_VALUE = -1e30

COL_NA_Q = 0
COL_NA_K = NA_WIDTH
COL_NA_V = 2 * NA_WIDTH
COL_POOL = 3 * NA_WIDTH
COL_ML_Q = COL_POOL + POOL_WIDTH
COL_ML_K = COL_ML_Q + ML_WIDTH
COL_ML_V = COL_ML_K + ML_WIDTH
COL_ML_O = COL_ML_V + ML_WIDTH
MAIN_COLS = COL_ML_O + ML_WIDTH
GATE_COLS = 4 * ML_HEADS
GATE_PAD = 128

MOD_ROWS = 8
VMEM_LIMIT = 56 * 1024 * 1024


def _params(sem, vmem=VMEM_LIMIT):
    return pltpu.CompilerParams(dimension_semantics=sem, vmem_limit_bytes=vmem)


def _modulation_kernel(c_ref, w_ref, b_ref, o_ref):
    c = c_ref[...]
    s = (c * jax.nn.sigmoid(c)).astype(BF16)
    o_ref[...] = jnp.dot(s, w_ref[...].astype(BF16), preferred_element_type=F32) + b_ref[...]


def _modulation(cond, w_ada, b_ada, bn=512):
    n = w_ada.shape[-1]
    return pl.pallas_call(
        _modulation_kernel,
        out_shape=jax.ShapeDtypeStruct((DEPTH, MOD_ROWS, n), F32),
        grid=(DEPTH, n // bn),
        in_specs=[
            pl.BlockSpec((MOD_ROWS, D_MODEL), lambda l, j: (0, 0)),
            pl.BlockSpec((None, D_MODEL, bn), lambda l, j: (l, 0, j)),
            pl.BlockSpec((None, 1, bn), lambda l, j: (l, 0, j)),
        ],
        out_specs=pl.BlockSpec((None, MOD_ROWS, bn), lambda l, j: (l, 0, j)),
        compiler_params=_params(("parallel", "parallel")),
        name="modulation",
    )(cond, w_ada, b_ada)


def _norm_mod_kernel(x_ref, g_ref, mod_ref, o_ref, *, shift_row):
    x = x_ref[...]
    y = x * lax.rsqrt(jnp.mean(x * x, axis=-1, keepdims=True) + EPS)
    y = y * g_ref[...]
    shift = mod_ref[shift_row:shift_row + 1, :]
    scale = mod_ref[shift_row + 1:shift_row + 2, :]
    o_ref[...] = (y * (1 + scale) + shift).astype(o_ref.dtype)


def _final_norm_kernel(x_ref, g_ref, o_ref):
    x = x_ref[...]
    y = x * lax.rsqrt(jnp.mean(x * x, axis=-1, keepdims=True) + EPS)
    o_ref[...] = y * g_ref[...]


class _Groups:
    def __init__(self, m_ctx, dec_seq):
        self.m_ctx = m_ctx
        self.dec_seq = dec_seq

    def of_block(self, i, bm):
        assert self.m_ctx % bm == 0 and self.dec_seq % bm == 0
        return jnp.maximum((i * bm) // self.dec_seq - (self.m_ctx // self.dec_seq - 1), 0)


def _norm_mod(x, g, mod, groups, shift_row, bm=256):
    m, d = x.shape
    return pl.pallas_call(
        functools.partial(_norm_mod_kernel, shift_row=shift_row),
        out_shape=jax.ShapeDtypeStruct((m, d), BF16),
        grid=(m // bm,),
        in_specs=[
            pl.BlockSpec((bm, d), lambda i: (i, 0)),
            pl.BlockSpec((1, d), lambda i: (0, 0)),
            pl.BlockSpec((None, N_MOD, d), lambda i: (groups.of_block(i, bm), 0, 0)),
        ],
        out_specs=pl.BlockSpec((bm, d), lambda i: (i, 0)),
        compiler_params=_params(("parallel",)),
        name="norm_mod",
    )(x, g, mod)


def _final_norm(x, g, bm=256):
    m, d = x.shape
    return pl.pallas_call(
        _final_norm_kernel,
        out_shape=jax.ShapeDtypeStruct((m, d), F32),
        grid=(m // bm,),
        in_specs=[pl.BlockSpec((bm, d), lambda i: (i, 0)), pl.BlockSpec((1, d), lambda i: (0, 0))],
        out_specs=pl.BlockSpec((bm, d), lambda i: (i, 0)),
        compiler_params=_params(("parallel",)),
        name="final_norm",
    )(x, g)


def _mm_plain_kernel(h_ref, w_ref, o_ref):
    o_ref[...] = jnp.dot(h_ref[...], w_ref[...], preferred_element_type=F32)


def _mm_bias_kernel(h_ref, w_ref, b_ref, o_ref):
    o_ref[...] = jnp.dot(h_ref[...], w_ref[...], preferred_element_type=F32) + b_ref[...]


def _mm_swiglu_kernel(h_ref, wa_ref, wb_ref, o_ref):
    h = h_ref[...]
    a = jnp.dot(h, wa_ref[...], preferred_element_type=F32)
    b = jnp.dot(h, wb_ref[...], preferred_element_type=F32)
    o_ref[...] = ((a * jax.nn.sigmoid(a)) * b).astype(o_ref.dtype)


def _mm_resid_kernel(a_ref, w_ref, x_ref, mod_ref, o_ref, *, gate_row, coef):
    y = jnp.dot(a_ref[...], w_ref[...], preferred_element_type=F32)
    gate = mod_ref[gate_row:gate_row + 1, :]
    o_ref[...] = x_ref[...] + (coef * gate) * y


def _mm_merge_kernel(h_ref, na_ref, po_ref, ml_ref, wg_na, wg_po, wg_ml, wb_na, wb_po, wb_ml, o_ref):
    h = h_ref[...]

    def branch(x_ref, wg_ref, wb_ref):
        g = jnp.dot(h, wg_ref[...], preferred_element_type=F32)
        y = jnp.dot(x_ref[...], wb_ref[...], preferred_element_type=F32)
        return jax.nn.sigmoid(g) * y

    o = branch(na_ref, wg_na, wb_na) + branch(po_ref, wg_po, wb_po) + branch(ml_ref, wg_ml, wb_ml)
    o_ref[...] = o.astype(o_ref.dtype)


def _mm_plain(h, w, bm=1024, bn=512):
    m, k = h.shape
    n = w.shape[1]
    return pl.pallas_call(
        _mm_plain_kernel,
        out_shape=jax.ShapeDtypeStruct((m, n), F32),
        grid=(m // bm, n // bn),
        in_specs=[pl.BlockSpec((bm, k), lambda i, j: (i, 0)), pl.BlockSpec((k, bn), lambda i, j: (0, j))],
        out_specs=pl.BlockSpec((bm, bn), lambda i, j: (i, j)),
        compiler_params=_params(("parallel", "arbitrary")),
        name="mm_proj",
    )(h, w)


def _mm_bias(h, w, b, bm=1024):
    m, k = h.shape
    n = w.shape[1]
    return pl.pallas_call(
        _mm_bias_kernel,
        out_shape=jax.ShapeDtypeStruct((m, n), F32),
        grid=(m // bm,),
        in_specs=[pl.BlockSpec((bm, k), lambda i: (i, 0)), pl.BlockSpec((k, n), lambda i: (0, 0)),
                  pl.BlockSpec((1, n), lambda i: (0, 0))],
        out_specs=pl.BlockSpec((bm, n), lambda i: (i, 0)),
        compiler_params=_params(("parallel",)),
        name="mm_gates",
    )(h, w, b)


def _mm_swiglu(h, w, bm=1024, bn=256):
    m, k = h.shape
    nb = D_FF // bn
    return pl.pallas_call(
        _mm_swiglu_kernel,
        out_shape=jax.ShapeDtypeStruct((m, D_FF), BF16),
        grid=(m // bm, nb),
        in_specs=[pl.BlockSpec((bm, k), lambda i, j: (i, 0)),
                  pl.BlockSpec((k, bn), lambda i, j: (0, j)),
                  pl.BlockSpec((k, bn), lambda i, j: (0, j + nb))],
        out_specs=pl.BlockSpec((bm, bn), lambda i, j: (i, j)),
        compiler_params=_params(("parallel", "arbitrary")),
        name="mm_swiglu",
    )(h, w, w)


def _mm_resid(a, w, x, mod, groups, gate_row, coef, bm, bn=256, single_buffer_act=False):
    m, k = a.shape
    n = w.shape[1]
    act_spec = (pl.BlockSpec((bm, k), lambda i, j: (i, 0), pipeline_mode=pl.Buffered(1))
                if single_buffer_act else pl.BlockSpec((bm, k), lambda i, j: (i, 0)))
    return pl.pallas_call(
        functools.partial(_mm_resid_kernel, gate_row=gate_row, coef=coef),
        out_shape=jax.ShapeDtypeStruct((m, n), F32),
        grid=(m // bm, n // bn),
        in_specs=[act_spec,
                  pl.BlockSpec((k, bn), lambda i, j: (0, j)),
                  pl.BlockSpec((bm, bn), lambda i, j: (i, j)),
                  pl.BlockSpec((None, N_MOD, bn), lambda i, j: (groups.of_block(i, bm), 0, j))],
        out_specs=pl.BlockSpec((bm, bn), lambda i, j: (i, j)),
        compiler_params=_params(("parallel", "arbitrary")),
        name="mm_resid",
    )(a, w, x, mod)


def _mm_merge(h, na, po, ml, wg, wb, bm=512, bn=256):
    m, d = h.shape
    nb = d // bn
    row = lambda i, j: (i, 0)
    return pl.pallas_call(
        _mm_merge_kernel,
        out_shape=jax.ShapeDtypeStruct((m, d), BF16),
        grid=(m // bm, nb),
        in_specs=[pl.BlockSpec((bm, d), row),
                  pl.BlockSpec((bm, NA_WIDTH), row),
                  pl.BlockSpec((bm, POOL_WIDTH), row),
                  pl.BlockSpec((bm, ML_WIDTH), row),
                  pl.BlockSpec((d, bn), lambda i, j: (0, j)),
                  pl.BlockSpec((d, bn), lambda i, j: (0, j + nb)),
                  pl.BlockSpec((d, bn), lambda i, j: (0, j + 2 * nb)),
                  pl.BlockSpec((NA_WIDTH, bn), lambda i, j: (0, j)),
                  pl.BlockSpec((POOL_WIDTH, bn), lambda i, j: (NA_WIDTH // POOL_WIDTH, j)),
                  pl.BlockSpec((ML_WIDTH, bn), lambda i, j: ((NA_WIDTH + POOL_WIDTH) // ML_WIDTH, j))],
        out_specs=pl.BlockSpec((bm, bn), lambda i, j: (i, j)),
        compiler_params=_params(("parallel", "arbitrary")),
        name="mm_merge",
    )(h, na, po, ml, wg, wg, wg, wb, wb, wb)


def _softmax_rows(parts):
    m = parts[0].max(axis=-1, keepdims=True)
    for s in parts[1:]:
        m = jnp.maximum(m, s.max(axis=-1, keepdims=True))
    es = [jnp.exp(s - m) for s in parts]
    den = es[0].sum(axis=-1, keepdims=True)
    for e in es[1:]:
        den = den + e.sum(axis=-1, keepdims=True)
    return [e / den for e in es]


def _dot_nt(a, b):
    return lax.dot_general(a, b, (((1,), (1,)), ((), ())), preferred_element_type=F32)


def _ctx_attn_kernel(q_ref, k_ref, v_ref, o_ref):
    scale = NA_HEAD_DIM ** -0.5
    for h in range(NA_HEADS):
        sl = slice(h * NA_HEAD_DIM, (h + 1) * NA_HEAD_DIM)
        q = q_ref[:, sl].astype(BF16)
        k = k_ref[:, sl].astype(BF16)
        v = v_ref[:, sl].astype(BF16)
        (p,) = _softmax_rows([_dot_nt(q, k) * scale])
        o_ref[:, sl] = jnp.dot(p.astype(BF16), v, preferred_element_type=F32).astype(o_ref.dtype)


def _ctx_attention(proj, n_seq, seq):
    blk = lambda c: pl.BlockSpec((seq, NA_WIDTH), lambda b: (b, c // NA_WIDTH))
    return pl.pallas_call(
        _ctx_attn_kernel,
        out_shape=jax.ShapeDtypeStruct((n_seq * seq, NA_WIDTH), BF16),
        grid=(n_seq,),
        in_specs=[blk(COL_NA_Q), blk(COL_NA_K), blk(COL_NA_V)],
        out_specs=pl.BlockSpec((seq, NA_WIDTH), lambda b: (b, 0)),
        compiler_params=_params(("parallel",)),
        name="ctx_attention",
    )(proj, proj, proj)


def _na_attn_kernel(q_ref, k_ref, v_ref, ck_ref, cv_ref, bias_ref, o_ref, *, rows):
    r = pl.program_id(1)
    kr = min(NA_WIN_ROWS, rows)
    n_loc = kr * GRID_W
    start = jnp.clip(r - kr // 2, 0, rows - kr)
    k0 = pl.multiple_of(start * GRID_W, GRID_W)
    dr0 = start - r + (NA_WIN_ROWS - 1)
    scale = NA_HEAD_DIM ** -0.5
    wq = lax.broadcasted_iota(jnp.int32, (GRID_W, n_loc), 0)
    xk = lax.broadcasted_iota(jnp.int32, (GRID_W, n_loc), 1) % GRID_W
    cs = jnp.clip(wq - NA_WIN_COLS // 2, 0, GRID_W - NA_WIN_COLS)
    col_mask = (xk >= cs) & (xk < cs + NA_WIN_COLS)
    for h in range(NA_HEADS):
        sl = slice(h * NA_HEAD_DIM, (h + 1) * NA_HEAD_DIM)
        q = q_ref[:, sl].astype(BF16)
        kl = k_ref[pl.ds(k0, n_loc), sl].astype(BF16)
        vl = v_ref[pl.ds(k0, n_loc), sl].astype(BF16)
        bias = jnp.concatenate([bias_ref[h, dr0 + i] for i in range(kr)], axis=-1)
        s_loc = _dot_nt(q, kl) * scale + bias
        s_loc = jnp.where(col_mask, s_loc, MASK_VALUE)
        s_ctx = _dot_nt(q, ck_ref[:, sl].astype(BF16)) * scale
        p_loc, p_ctx = _softmax_rows([s_loc, s_ctx])
        out = (jnp.dot(p_loc.astype(BF16), vl, preferred_element_type=F32)
               + jnp.dot(p_ctx.astype(BF16), cv_ref[:, sl].astype(BF16), preferred_element_type=F32))
        o_ref[:, sl] = out.astype(o_ref.dtype)


def _na_attention(proj, row0, n_seq, seq, ctx_k, ctx_v, layer, bias_tab):
    rows = seq // GRID_W
    past = ctx_k.shape[2]
    q_spec = pl.BlockSpec((GRID_W, NA_WIDTH), lambda b, r: (row0 // GRID_W + b * rows + r, COL_NA_Q // NA_WIDTH))
    kv = lambda c: pl.BlockSpec((seq, NA_WIDTH), lambda b, r: (row0 // seq + b, c // NA_WIDTH))
    ctx = pl.BlockSpec((None, None, past, NA_WIDTH), lambda b, r: (b, layer, 0, 0))
    return pl.pallas_call(
        functools.partial(_na_attn_kernel, rows=rows),
        out_shape=jax.ShapeDtypeStruct((n_seq * seq, NA_WIDTH), BF16),
        grid=(n_seq, rows),
        in_specs=[q_spec, kv(COL_NA_K), kv(COL_NA_V), ctx, ctx,
                  pl.BlockSpec(bias_tab.shape, lambda b, r: (0, 0, 0, 0))],
        out_specs=pl.BlockSpec((GRID_W, NA_WIDTH), lambda b, r: (b * rows + r, 0)),
        compiler_params=_params(("parallel", "arbitrary")),
        name="na_attention",
    )(proj, proj, proj, ctx_k, ctx_v, bias_tab)


def _rel_bias_table(rel_bias):
    cq = np.arange(GRID_W)
    dc = np.clip(cq[None, :] - cq[:, None], -(NA_WIN_COLS - 1), NA_WIN_COLS - 1) + (NA_WIN_COLS - 1)
    return rel_bias[:, :, dc]


def _pool_kernel(u_ref, w_ref, s_ref, o_ref, *, seq):
    pos = lax.broadcasted_iota(jnp.int32, (seq, POOL_GROUP), 0)
    for g, win in enumerate(POOL_WINDOWS):
        sl = slice(g * POOL_GROUP, (g + 1) * POOL_GROUP)
        u = u_ref[:, sl]
        acc = jnp.zeros_like(u)
        for d in range(-(win // 2), win - win // 2):
            shifted = u if d == 0 else pltpu.roll(u, (-d) % seq, 0)
            valid = (pos + d >= 0) & (pos + d < seq)
            acc = acc + jnp.where(valid, shifted, 0.0)
        lo = jnp.clip(pos - win // 2, 0, seq)
        hi = jnp.clip(pos + win - win // 2, 0, seq)
        pooled = acc / (hi - lo).astype(F32) - u
        y = jnp.dot(pooled.astype(BF16), w_ref[g].astype(BF16), preferred_element_type=F32)
        o_ref[:, sl] = (y * s_ref[:, sl]).astype(o_ref.dtype)


def _pool(proj, row0, n_seq, seq, pool_w, pool_scale):
    return pl.pallas_call(
        functools.partial(_pool_kernel, seq=seq),
        out_shape=jax.ShapeDtypeStruct((n_seq * seq, POOL_WIDTH), BF16),
        grid=(n_seq,),
        in_specs=[pl.BlockSpec((seq, POOL_WIDTH), lambda b: (row0 // seq + b, COL_POOL // POOL_WIDTH)),
                  pl.BlockSpec(pool_w.shape, lambda b: (0, 0, 0)),
                  pl.BlockSpec((1, POOL_WIDTH), lambda b: (0, 0))],
        out_specs=pl.BlockSpec((seq, POOL_WIDTH), lambda b: (b, 0)),
        compiler_params=_params(("parallel",)),
        name="pool",
    )(proj, pool_w, pool_scale)


def _log_sigmoid(x):
    return jnp.minimum(x, 0.0) - jnp.log1p(jnp.exp(-jnp.abs(x)))


def _mlstm_chunk(q, k, v, i_col, f_col, c_prev, n_prev, m_prev, reverse):
    L = q.shape[0]
    t_idx = lax.broadcasted_iota(jnp.int32, (L, L), 0)
    s_idx = lax.broadcasted_iota(jnp.int32, (L, L), 1)
    eye = t_idx == s_idx
    vis = (s_idx >= t_idx) if reverse else (s_idx <= t_idx)
    vis_t = (t_idx >= s_idx) if reverse else (t_idx <= s_idx)
    f_row = jnp.sum(jnp.where(eye, f_col, 0.0), axis=0, keepdims=True)
    i_row = jnp.sum(jnp.where(eye, i_col, 0.0), axis=0, keepdims=True)
    b_col = jnp.sum(jnp.where(vis, f_row, 0.0), axis=1, keepdims=True)
    b_row = jnp.sum(jnp.where(vis_t, f_col, 0.0), axis=0, keepdims=True)
    total = jnp.sum(f_col, axis=0, keepdims=True)
    dmat = jnp.where(vis, b_col - b_row + i_row, -jnp.inf)
    inter = b_col + m_prev
    m_t = jnp.maximum(inter, jnp.max(dmat, axis=1, keepdims=True))
    w_inter = jnp.exp(inter - m_t)
    ks = k * (ML_HEAD_DIM ** -0.5)
    qb = q.astype(BF16)
    vb = v.astype(BF16)
    qk = _dot_nt(qb, ks.astype(BF16)) * jnp.exp(dmat - m_t)
    num = (jnp.dot(qk.astype(BF16), vb, preferred_element_type=F32)
           + w_inter * jnp.dot(qb, c_prev.astype(BF16), preferred_element_type=F32))
    den = jnp.sum(qk, axis=1, keepdims=True) + w_inter * jnp.sum(q * n_prev, axis=1, keepdims=True)
    h = num / jnp.maximum(jnp.abs(den), jnp.exp(-m_t))
    g = total - b_col + i_col
    m_new = jnp.maximum(total + m_prev, jnp.max(g, axis=0, keepdims=True))
    ws = jnp.exp(g - m_new)
    decay = jnp.exp(total + m_prev - m_new)
    kw = ks * ws
    c_new = decay * c_prev + lax.dot_general(kw.astype(BF16), vb, (((0,), (0,)), ((), ())),
                                             preferred_element_type=F32)
    n_new = decay * n_prev + jnp.sum(kw, axis=0, keepdims=True)
    return h, c_new, n_new, m_new


def _mlstm_kernel(*refs, has_init, emit_state):
    qf, kf, vf, gf, qb, kb, vb, gb = refs[:8]
    pos = 8
    if has_init:
        c0_ref, n0_ref, m0_ref = refs[pos:pos + 3]
        pos += 3
    hf_ref, hb_ref = refs[pos:pos + 2]
    pos += 2
    if emit_state:
        co_ref, no_ref, mo_ref = refs[pos:pos + 3]
        pos += 3
    c_s, n_s, m_s = refs[pos:pos + 3]
    head = pl.program_id(1)
    c = pl.program_id(2)

    @pl.when(c == 0)
    def _():
        if has_init:
            c_s[...] = c0_ref[...]
            n_s[...] = n0_ref[...]
            m_s[...] = m0_ref[...]
        else:
            c_s[...] = jnp.zeros_like(c_s)
            n_s[...] = jnp.zeros_like(n_s)
            m_s[...] = jnp.zeros_like(m_s)

    lane = lax.broadcasted_iota(jnp.int32, (ML_CHUNK, GATE_PAD), 1)

    def gate_col(g_ref, kind):
        return jnp.sum(jnp.where(lane == head + kind * ML_HEADS, g_ref[...], 0.0), axis=1, keepdims=True)

    for d, (q_ref, k_ref, v_ref, g_ref, h_ref) in enumerate(((qf, kf, vf, gf, hf_ref), (qb, kb, vb, gb, hb_ref))):
        i_col = gate_col(g_ref, 2 * d)
        f_col = _log_sigmoid(gate_col(g_ref, 2 * d + 1))
        h, c_new, n_new, m_new = _mlstm_chunk(q_ref[...], k_ref[...], v_ref[...], i_col, f_col,
                                              c_s[d], n_s[d], m_s[d], reverse=(d == 1))
        h_ref[...] = h
        c_s[d] = c_new
        n_s[d] = n_new
        m_s[d] = m_new

    if emit_state:
        @pl.when(c == pl.num_programs(2) - 1)
        def _():
            co_ref[...] = c_s[...]
            no_ref[...] = n_s[...]
            mo_ref[...] = m_s[...]


def _mlstm(proj, gates, row0, n_seq, seq, init=None, layer=0, emit_state=False):
    nc = seq // ML_CHUNK
    dh = ML_HEAD_DIM
    rb = row0 // ML_CHUNK

    def tok(col, rev):
        def imap(b, h, c):
            cc = (nc - 1 - c) if rev else c
            return (rb + b * nc + cc, col // dh + h)
        return pl.BlockSpec((ML_CHUNK, dh), imap)

    def gat(rev):
        def imap(b, h, c):
            cc = (nc - 1 - c) if rev else c
            return (rb + b * nc + cc, 0)
        return pl.BlockSpec((ML_CHUNK, GATE_PAD), imap)

    def hout(rev):
        def imap(b, h, c):
            cc = (nc - 1 - c) if rev else c
            return (b * nc + cc, h)
        return pl.BlockSpec((ML_CHUNK, dh), imap)

    in_specs = [tok(COL_ML_Q, False), tok(COL_ML_K, False), tok(COL_ML_V, False), gat(False),
                tok(COL_ML_Q, True), tok(COL_ML_K, True), tok(COL_ML_V, True), gat(True)]
    args = [proj, proj, proj, gates, proj, proj, proj, gates]
    if init is not None:
        c0, n0, m0 = init
        in_specs += [pl.BlockSpec((None, None, 2, None, dh, dh), lambda b, h, c: (b, layer, 0, h, 0, 0)),
                     pl.BlockSpec((None, None, 2, None, 1, dh), lambda b, h, c: (b, layer, 0, h, 0, 0)),
                     pl.BlockSpec((None, None, 2, None, 1, 1), lambda b, h, c: (b, layer, 0, h, 0, 0))]
        args += [c0, n0, m0]
    out_shape = [jax.ShapeDtypeStruct((n_seq * seq, ML_WIDTH), F32)] * 2
    out_specs = [hout(False), hout(True)]
    if emit_state:
        out_shape += [jax.ShapeDtypeStruct((n_seq, 2, ML_HEADS, dh, dh), F32),
                      jax.ShapeDtypeStruct((n_seq, 2, ML_HEADS, 1, dh), F32),
                      jax.ShapeDtypeStruct((n_seq, 2, ML_HEADS, 1, 1), F32)]
        out_specs += [pl.BlockSpec((None, 2, None, dh, dh), lambda b, h, c: (b, 0, h, 0, 0)),
                      pl.BlockSpec((None, 2, None, 1, dh), lambda b, h, c: (b, 0, h, 0, 0)),
                      pl.BlockSpec((None, 2, None, 1, 1), lambda b, h, c: (b, 0, h, 0, 0))]
    return pl.pallas_call(
        functools.partial(_mlstm_kernel, has_init=init is not None, emit_state=emit_state),
        out_shape=out_shape,
        grid=(n_seq, ML_HEADS, nc),
        in_specs=in_specs,
        out_specs=out_specs,
        scratch_shapes=[pltpu.VMEM((2, dh, dh), F32), pltpu.VMEM((2, 1, dh), F32), pltpu.VMEM((2, 1, 1), F32)],
        compiler_params=_params(("parallel", "parallel", "arbitrary")),
        name="mlstm",
    )(*args)


def _ml_post_kernel(hf_ref, hb_ref, o_ref, g_ref, out_ref):
    for h in range(ML_HEADS):
        sl = slice(h * ML_HEAD_DIM, (h + 1) * ML_HEAD_DIM)
        x = hf_ref[:, sl] + hb_ref[:, sl]
        x = x * lax.rsqrt(jnp.mean(x * x, axis=-1, keepdims=True) + EPS)
        x = x * g_ref[:, sl]
        out_ref[:, sl] = (jax.nn.sigmoid(o_ref[:, sl]) * x).astype(out_ref.dtype)


def _ml_post(hf, hb, proj, norm_g, bm=256):
    m = hf.shape[0]
    row = lambda i: (i, 0)
    return pl.pallas_call(
        _ml_post_kernel,
        out_shape=jax.ShapeDtypeStruct((m, ML_WIDTH), BF16),
        grid=(m // bm,),
        in_specs=[pl.BlockSpec((bm, ML_WIDTH), row), pl.BlockSpec((bm, ML_WIDTH), row),
                  pl.BlockSpec((bm, ML_WIDTH), lambda i: (i, COL_ML_O // ML_WIDTH)),
                  pl.BlockSpec((1, ML_WIDTH), lambda i: (0, 0))],
        out_specs=pl.BlockSpec((bm, ML_WIDTH), row),
        compiler_params=_params(("parallel",)),
        name="ml_post",
    )(hf, hb, proj, norm_g)


def kernel(x_prompt, x_sample, c, cache_na_k, cache_na_v, state_mlstm_C, state_mlstm_n, state_mlstm_m, c_ctx, w_ada, b_ada, norm_g, ffn_w_in, ffn_w_out, w_in, na_rel_bias, pool_w, pool_scale, ml_gate_bias, ml_norm_g, w_branch, w_out, final_norm_g):
    batch, seq, d = x_prompt.shape
    dec_batch, dec_seq, _ = x_sample.shape
    past = cache_na_k.shape[2]
    assert d == D_MODEL and dec_batch + 1 <= MOD_ROWS
    m_ctx = batch * seq
    m_lat = dec_batch * dec_seq
    groups = _Groups(m_ctx, dec_seq)

    x = jnp.concatenate([x_prompt.reshape(m_ctx, d), x_sample.reshape(m_lat, d)], axis=0)
    cond = jnp.concatenate([c_ctx[None], c, jnp.zeros((MOD_ROWS - 1 - dec_batch, d), F32)], axis=0)
    mod_all = _modulation(cond, w_ada, b_ada.reshape(DEPTH, 1, N_MOD * d))
    mod_all = mod_all.reshape(DEPTH, MOD_ROWS, N_MOD, d)

    ctx_k = cache_na_k.reshape(dec_batch, DEPTH, past, NA_WIDTH)
    ctx_v = cache_na_v.reshape(dec_batch, DEPTH, past, NA_WIDTH)
    init = (state_mlstm_C,
            state_mlstm_n.reshape(dec_batch, DEPTH, 2, ML_HEADS, 1, ML_HEAD_DIM),
            state_mlstm_m.reshape(dec_batch, DEPTH, 2, ML_HEADS, 1, 1))

    ks_out, vs_out, cs_out, ns_out, ms_out = [], [], [], [], []
    for l in range(DEPTH):
        mod = mod_all[l]
        w_ffn_in = [ffn_w_in[l, j].astype(BF16) for j in range(2)]
        w_ffn_out = [ffn_w_out[l, j].astype(BF16) for j in range(2)]
        w_main = w_in[l, :, :MAIN_COLS].astype(BF16)
        w_gate = jnp.pad(w_in[l, :, MAIN_COLS:MAIN_COLS + GATE_COLS], ((0, 0), (0, GATE_PAD - GATE_COLS))).astype(BF16)
        b_gate = jnp.pad(ml_gate_bias[l], (0, GATE_PAD - GATE_COLS)).reshape(1, GATE_PAD)
        w_merge = w_in[l, :, MAIN_COLS + GATE_COLS:].astype(BF16)
        w_br = w_branch[l].astype(BF16)
        w_o = w_out[l].astype(BF16)

        def ffn(x, j, shift_row):
            h = _norm_mod(x, norm_g[l, 2 * j].reshape(1, d), mod, groups, shift_row)
            act = _mm_swiglu(h, w_ffn_in[j])
            return _mm_resid(act, w_ffn_out[j], x, mod, groups, shift_row + 2, MACARON_W,
                             bm=1024, single_buffer_act=True)

        x = ffn(x, 0, 0)

        h = _norm_mod(x, norm_g[l, 1].reshape(1, d), mod, groups, 3)
        proj = _mm_plain(h, w_main)
        gates = _mm_bias(h, w_gate, b_gate)

        bias_tab = _rel_bias_table(na_rel_bias[l])
        na_out = jnp.concatenate([
            _ctx_attention(proj, batch, seq),
            _na_attention(proj, m_ctx, dec_batch, dec_seq, ctx_k, ctx_v, l, bias_tab)], axis=0)
        ps = pool_scale[l].reshape(1, POOL_WIDTH)
        pool_out = jnp.concatenate([
            _pool(proj, 0, batch, seq, pool_w[l], ps),
            _pool(proj, m_ctx, dec_batch, dec_seq, pool_w[l], ps)], axis=0)
        hf_c, hb_c, c_fin, n_fin, m_fin = _mlstm(proj, gates, 0, batch, seq, emit_state=True)
        hf_l, hb_l = _mlstm(proj, gates, m_ctx, dec_batch, dec_seq, init=init, layer=l)
        ml_out = _ml_post(jnp.concatenate([hf_c, hf_l], axis=0), jnp.concatenate([hb_c, hb_l], axis=0),
                          proj, ml_norm_g[l].reshape(1, ML_WIDTH))

        merged = _mm_merge(h, na_out, pool_out, ml_out, w_merge, w_br)
        x = _mm_resid(merged, w_o, x, mod, groups, 5, 1.0, bm=1024)

        x = ffn(x, 1, 6)

        ks_out.append(proj[:m_ctx, COL_NA_K:COL_NA_K + NA_WIDTH].reshape(batch, seq, NA_HEADS, NA_HEAD_DIM))
        vs_out.append(proj[:m_ctx, COL_NA_V:COL_NA_V + NA_WIDTH].reshape(batch, seq, NA_HEADS, NA_HEAD_DIM))
        cs_out.append(c_fin)
        ns_out.append(n_fin.reshape(batch, 2, ML_HEADS, ML_HEAD_DIM))
        ms_out.append(m_fin.reshape(batch, 2, ML_HEADS))

    y = _final_norm(x, final_norm_g.reshape(1, d))
    y_prompt = y[:m_ctx].reshape(batch, seq, d)
    y_sample = y[m_ctx:].reshape(dec_batch, dec_seq, d)
    return (y_prompt, y_sample, jnp.stack(ks_out, axis=1), jnp.stack(vs_out, axis=1),
            jnp.stack(cs_out, axis=1), jnp.stack(ns_out, axis=1), jnp.stack(ms_out, axis=1))
```

```python
import functools

import jax
import jax.numpy as jnp
import numpy as np
from jax import lax
from jax.experimental import pallas as pl
from jax.experimental.pallas import tpu as pltpu

F32 = jnp.float32
BF16 = jnp.bfloat16

D_MODEL = 4096
DEPTH = 2
GRID_W = 64
NA_HEADS = 8
NA_WIDTH = D_MODEL // 4
NA_HEAD_DIM = NA_WIDTH // NA_HEADS
NA_WIN_ROWS = 8
NA_WIN_COLS = 16
POOL_WINDOWS = (2, 4, 8, 16)
POOL_WIDTH = D_MODEL // 4
POOL_GROUP = POOL_WIDTH // 4
ML_HEADS = 8
ML_WIDTH = D_MODEL // 2
ML_HEAD_DIM = ML_WIDTH // ML_HEADS
ML_CHUNK = 64
D_FF = 256 * ((8 * D_MODEL // 3 + 255) // 256)
N_MOD = 9
MACARON_W = 0.5
EPS = 1e-6
MASK_VALUE = -1e30

COL_NA_Q = 0
COL_NA_K = NA_WIDTH
COL_NA_V = 2 * NA_WIDTH
COL_POOL = 3 * NA_WIDTH
COL_ML_Q = COL_POOL + POOL_WIDTH
COL_ML_K = COL_ML_Q + ML_WIDTH
COL_ML_V = COL_ML_K + ML_WIDTH
COL_ML_O = COL_ML_V + ML_WIDTH
MAIN_COLS = COL_ML_O + ML_WIDTH
GATE_COLS = 4 * ML_HEADS
GATE_PAD = 128

MOD_ROWS = 8
VMEM_LIMIT = 56 * 1024 * 1024


def _params(sem, vmem=VMEM_LIMIT):
    return pltpu.CompilerParams(dimension_semantics=sem, vmem_limit_bytes=vmem)


def _resident(block_shape, index_map):
    return pl.BlockSpec(block_shape, index_map, pipeline_mode=pl.Buffered(1))


_ANY = pl.BlockSpec(memory_space=pl.ANY)


def _modulation_kernel(c_ref, w_ref, b_ref, o_ref):
    c = c_ref[...]
    s = (c * jax.nn.sigmoid(c)).astype(BF16)
    o_ref[...] = jnp.dot(s, w_ref[...].astype(BF16), preferred_element_type=F32) + b_ref[...]


def _modulation(cond, w_ada, b_ada, bn=512):
    n = w_ada.shape[-1]
    return pl.pallas_call(
        _modulation_kernel,
        out_shape=jax.ShapeDtypeStruct((DEPTH, MOD_ROWS, n), F32),
        grid=(DEPTH, n // bn),
        in_specs=[
            pl.BlockSpec((MOD_ROWS, D_MODEL), lambda l, j: (0, 0)),
            pl.BlockSpec((None, D_MODEL, bn), lambda l, j: (l, 0, j)),
            pl.BlockSpec((None, 1, bn), lambda l, j: (l, 0, j)),
        ],
        out_specs=pl.BlockSpec((None, MOD_ROWS, bn), lambda l, j: (l, 0, j)),
        compiler_params=_params(("parallel", "parallel")),
        name="modulation",
    )(cond, w_ada, b_ada)


def _norm_mod_kernel(x_ref, g_ref, mod_ref, o_ref, *, shift_row):
    x = x_ref[...]
    y = x * lax.rsqrt(jnp.mean(x * x, axis=-1, keepdims=True) + EPS)
    y = y * g_ref[...]
    shift = mod_ref[shift_row:shift_row + 1, :]
    scale = mod_ref[shift_row + 1:shift_row + 2, :]
    o_ref[...] = (y * (1 + scale) + shift).astype(o_ref.dtype)


def _final_norm_kernel(x_ref, g_ref, o_ref):
    x = x_ref[...]
    y = x * lax.rsqrt(jnp.mean(x * x, axis=-1, keepdims=True) + EPS)
    o_ref[...] = y * g_ref[...]


class _Groups:
    def __init__(self, m_ctx, dec_seq):
        self.m_ctx = m_ctx
        self.dec_seq = dec_seq

    def of_block(self, i, bm):
        assert self.m_ctx % bm == 0 and self.dec_seq % bm == 0
        return jnp.maximum((i * bm) // self.dec_seq - (self.m_ctx // self.dec_seq - 1), 0)


def _norm_mod(x, g, mod, groups, shift_row, bm=256):
    m, d = x.shape
    return pl.pallas_call(
        functools.partial(_norm_mod_kernel, shift_row=shift_row),
        out_shape=jax.ShapeDtypeStruct((m, d), BF16),
        grid=(m // bm,),
        in_specs=[
            pl.BlockSpec((bm, d), lambda i: (i, 0)),
            pl.BlockSpec((1, d), lambda i: (0, 0)),
            pl.BlockSpec((None, N_MOD, d), lambda i: (groups.of_block(i, bm), 0, 0)),
        ],
        out_specs=pl.BlockSpec((bm, d), lambda i: (i, 0)),
        compiler_params=_params(("parallel",)),
        name="norm_mod",
    )(x, g, mod)


def _final_norm(x, row0, rows, g, bm=256):
    d = x.shape[1]
    return pl.pallas_call(
        _final_norm_kernel,
        out_shape=jax.ShapeDtypeStruct((rows, d), F32),
        grid=(rows // bm,),
        in_specs=[pl.BlockSpec((bm, d), lambda i: (row0 // bm + i, 0)), pl.BlockSpec((1, d), lambda i: (0, 0))],
        out_specs=pl.BlockSpec((bm, d), lambda i: (i, 0)),
        compiler_params=_params(("parallel",)),
        name="final_norm",
    )(x, g)


def _mm_plain_kernel(h_ref, w_ref, o_ref):
    o_ref[...] = jnp.dot(h_ref[...], w_ref[...].astype(BF16), preferred_element_type=F32)


def _mm_bias_kernel(h_ref, w_ref, b_ref, o_ref):
    o_ref[...] = jnp.dot(h_ref[...], w_ref[...].astype(BF16), preferred_element_type=F32) + b_ref[...]


def _mm_swiglu_kernel(h_ref, wa_ref, wb_ref, o_ref):
    h = h_ref[...]
    a = jnp.dot(h, wa_ref[...].astype(BF16), preferred_element_type=F32)
    b = jnp.dot(h, wb_ref[...].astype(BF16), preferred_element_type=F32)
    o_ref[...] = ((a * jax.nn.sigmoid(a)) * b).astype(o_ref.dtype)


def _mm_resid_kernel(a_ref, w_ref, x_ref, mod_ref, o_ref, *, gate_row, coef):
    y = jnp.dot(a_ref[...], w_ref[...].astype(BF16), preferred_element_type=F32)
    gate = mod_ref[gate_row:gate_row + 1, :]
    o_ref[...] = x_ref[...] + (coef * gate) * y


def _mm_merge_kernel(h_ref, na_ref, po_ref, ml_ref, wg_na, wg_po, wg_ml, wb_na, wb_po, wb_ml, o_ref):
    h = h_ref[...]

    def branch(x_ref, wg_ref, wb_ref):
        g = jnp.dot(h, wg_ref[...].astype(BF16), preferred_element_type=F32)
        y = jnp.dot(x_ref[...], wb_ref[...].astype(BF16), preferred_element_type=F32)
        return jax.nn.sigmoid(g) * y

    o = branch(na_ref, wg_na, wb_na) + branch(po_ref, wg_po, wb_po) + branch(ml_ref, wg_ml, wb_ml)
    o_ref[...] = o.astype(o_ref.dtype)


def _mm_proj(h, w_in, layer, bm=2048, bn=256):
    m, k = h.shape
    return pl.pallas_call(
        _mm_plain_kernel,
        out_shape=jax.ShapeDtypeStruct((m, MAIN_COLS), F32),
        grid=(m // bm, MAIN_COLS // bn),
        in_specs=[_resident((bm, k), lambda i, j: (i, 0)),
                  pl.BlockSpec((None, k, bn), lambda i, j: (layer, 0, j))],
        out_specs=pl.BlockSpec((bm, bn), lambda i, j: (i, j)),
        compiler_params=_params(("parallel", "arbitrary")),
        name="mm_proj",
    )(h, w_in)


def _mm_gates(h, w_in, layer, b, bm=1024):
    m, k = h.shape
    assert MAIN_COLS % GATE_PAD == 0
    return pl.pallas_call(
        _mm_bias_kernel,
        out_shape=jax.ShapeDtypeStruct((m, GATE_PAD), F32),
        grid=(m // bm,),
        in_specs=[pl.BlockSpec((bm, k), lambda i: (i, 0)),
                  pl.BlockSpec((None, k, GATE_PAD), lambda i: (layer, 0, MAIN_COLS // GATE_PAD)),
                  pl.BlockSpec((1, GATE_PAD), lambda i: (0, 0))],
        out_specs=pl.BlockSpec((bm, GATE_PAD), lambda i: (i, 0)),
        compiler_params=_params(("parallel",)),
        name="mm_gates",
    )(h, w_in, b)


def _mm_swiglu(h, ffn_w_in, layer, j, bm=2048, bn=256):
    m, k = h.shape
    nb = D_FF // bn
    return pl.pallas_call(
        _mm_swiglu_kernel,
        out_shape=jax.ShapeDtypeStruct((m, D_FF), BF16),
        grid=(m // bm, nb),
        in_specs=[_resident((bm, k), lambda i, n: (i, 0)),
                  pl.BlockSpec((None, None, k, bn), lambda i, n: (layer, j, 0, n)),
                  pl.BlockSpec((None, None, k, bn), lambda i, n: (layer, j, 0, n + nb))],
        out_specs=pl.BlockSpec((bm, bn), lambda i, n: (i, n)),
        compiler_params=_params(("parallel", "arbitrary")),
        name="mm_swiglu",
    )(h, ffn_w_in, ffn_w_in)


def _mm_resid(a, w, w_index, x, mod, groups, gate_row, coef, bm, bn=256):
    m, k = a.shape
    n = w.shape[-1]
    lead = (None,) * len(w_index)
    return pl.pallas_call(
        functools.partial(_mm_resid_kernel, gate_row=gate_row, coef=coef),
        out_shape=jax.ShapeDtypeStruct((m, n), F32),
        grid=(m // bm, n // bn),
        in_specs=[_resident((bm, k), lambda i, j: (i, 0)),
                  pl.BlockSpec(lead + (k, bn), lambda i, j: w_index + (0, j)),
                  pl.BlockSpec((bm, bn), lambda i, j: (i, j)),
                  pl.BlockSpec((None, N_MOD, bn), lambda i, j: (groups.of_block(i, bm), 0, j))],
        out_specs=pl.BlockSpec((bm, bn), lambda i, j: (i, j)),
        compiler_params=_params(("parallel", "arbitrary")),
        name="mm_resid",
    )(a, w, x, mod)


def _mm_merge(h, na, po, ml, wg, w_branch, layer, bm=1024, bn=256):
    m, d = h.shape
    nb = d // bn
    row = lambda i, j: (i, 0)
    return pl.pallas_call(
        _mm_merge_kernel,
        out_shape=jax.ShapeDtypeStruct((m, d), BF16),
        grid=(m // bm, nb),
        in_specs=[_resident((bm, d), row),
                  _resident((bm, NA_WIDTH), row),
                  _resident((bm, POOL_WIDTH), row),
                  _resident((bm, ML_WIDTH), row),
                  pl.BlockSpec((d, bn), lambda i, j: (0, j)),
                  pl.BlockSpec((d, bn), lambda i, j: (0, j + nb)),
                  pl.BlockSpec((d, bn), lambda i, j: (0, j + 2 * nb)),
                  pl.BlockSpec((None, NA_WIDTH, bn), lambda i, j: (layer, 0, j)),
                  pl.BlockSpec((None, POOL_WIDTH, bn), lambda i, j: (layer, NA_WIDTH // POOL_WIDTH, j)),
                  pl.BlockSpec((None, ML_WIDTH, bn), lambda i, j: (layer, (NA_WIDTH + POOL_WIDTH) // ML_WIDTH, j))],
        out_specs=pl.BlockSpec((bm, bn), lambda i, j: (i, j)),
        compiler_params=_params(("parallel", "arbitrary")),
        name="mm_merge",
    )(h, na, po, ml, wg, wg, wg, w_branch, w_branch, w_branch)


def _softmax_rows(parts):
    m = parts[0].max(axis=-1, keepdims=True)
    for s in parts[1:]:
        m = jnp.maximum(m, s.max(axis=-1, keepdims=True))
    es = [jnp.exp(s - m) for s in parts]
    den = es[0].sum(axis=-1, keepdims=True)
    for e in es[1:]:
        den = den + e.sum(axis=-1, keepdims=True)
    return [e / den for e in es]


def _dot_nt(a, b):
    return lax.dot_general(a, b, (((1,), (1,)), ((), ())), preferred_element_type=F32)


def _ctx_attn_kernel(q_ref, k_ref, v_ref, o_ref):
    scale = NA_HEAD_DIM ** -0.5
    for h in range(NA_HEADS):
        sl = slice(h * NA_HEAD_DIM, (h + 1) * NA_HEAD_DIM)
        q = q_ref[:, sl].astype(BF16)
        k = k_ref[:, sl].astype(BF16)
        v = v_ref[:, sl].astype(BF16)
        (p,) = _softmax_rows([_dot_nt(q, k) * scale])
        o_ref[:, sl] = jnp.dot(p.astype(BF16), v, preferred_element_type=F32).astype(o_ref.dtype)


def _ctx_attention(proj, n_seq, seq):
    blk = lambda c: pl.BlockSpec((seq, NA_WIDTH), lambda b: (b, c // NA_WIDTH))
    return pl.pallas_call(
        _ctx_attn_kernel,
        out_shape=jax.ShapeDtypeStruct((proj.shape[0], NA_WIDTH), BF16),
        grid=(n_seq,),
        in_specs=[blk(COL_NA_Q), blk(COL_NA_K), blk(COL_NA_V)],
        out_specs=pl.BlockSpec((seq, NA_WIDTH), lambda b: (b, 0)),
        compiler_params=_params(("parallel",)),
        name="ctx_attention",
    )(proj, proj, proj)


def _na_attn_kernel(q_ref, k_ref, v_ref, ck_ref, cv_ref, bias_ref, prev_ref, o_ref, *, rows):
    del prev_ref
    r = pl.program_id(1)
    kr = min(NA_WIN_ROWS, rows)
    n_loc = kr * GRID_W
    start = jnp.clip(r - kr // 2, 0, rows - kr)
    k0 = pl.multiple_of(start * GRID_W, GRID_W)
    dr0 = start - r + (NA_WIN_ROWS - 1)
    scale = NA_HEAD_DIM ** -0.5
    wq = lax.broadcasted_iota(jnp.int32, (GRID_W, n_loc), 0)
    xk = lax.broadcasted_iota(jnp.int32, (GRID_W, n_loc), 1) % GRID_W
    cs = jnp.clip(wq - NA_WIN_COLS // 2, 0, GRID_W - NA_WIN_COLS)
    col_mask = (xk >= cs) & (xk < cs + NA_WIN_COLS)
    for h in range(NA_HEADS):
        sl = slice(h * NA_HEAD_DIM, (h + 1) * NA_HEAD_DIM)
        q = q_ref[:, sl].astype(BF16)
        kl = k_ref[pl.ds(k0, n_loc), sl].astype(BF16)
        vl = v_ref[pl.ds(k0, n_loc), sl].astype(BF16)
        bias = jnp.concatenate([bias_ref[h, dr0 + i] for i in range(kr)], axis=-1)
        s_loc = _dot_nt(q, kl) * scale + bias
        s_loc = jnp.where(col_mask, s_loc, MASK_VALUE)
        s_ctx = _dot_nt(q, ck_ref[:, sl].astype(BF16)) * scale
        p_loc, p_ctx = _softmax_rows([s_loc, s_ctx])
        out = (jnp.dot(p_loc.astype(BF16), vl, preferred_element_type=F32)
               + jnp.dot(p_ctx.astype(BF16), cv_ref[:, sl].astype(BF16), preferred_element_type=F32))
        o_ref[:, sl] = out.astype(o_ref.dtype)


def _na_attention(proj, row0, n_seq, seq, ctx_k, ctx_v, layer, bias_tab, prev):
    rows = seq // GRID_W
    past = ctx_k.shape[2]
    rb = row0 // GRID_W
    q_spec = pl.BlockSpec((GRID_W, NA_WIDTH), lambda b, r: (rb + b * rows + r, COL_NA_Q // NA_WIDTH))
    kv = lambda c: _resident((seq, NA_WIDTH), lambda b, r: (row0 // seq + b, c // NA_WIDTH))
    ctx = _resident((None, None, past, NA_WIDTH), lambda b, r: (b, layer, 0, 0))
    return pl.pallas_call(
        functools.partial(_na_attn_kernel, rows=rows),
        out_shape=jax.ShapeDtypeStruct(prev.shape, prev.dtype),
        grid=(n_seq, rows),
        in_specs=[q_spec, kv(COL_NA_K), kv(COL_NA_V), ctx, ctx,
                  _resident(bias_tab.shape, lambda b, r: (0, 0, 0, 0)), _ANY],
        out_specs=pl.BlockSpec((GRID_W, NA_WIDTH), lambda b, r: (rb + b * rows + r, 0)),
        input_output_aliases={6: 0},
        compiler_params=_params(("parallel", "arbitrary")),
        name="na_attention",
    )(proj, proj, proj, ctx_k, ctx_v, bias_tab, prev)


def _rel_bias_table(rel_bias):
    cq = np.arange(GRID_W)
    dc = np.clip(cq[None, :] - cq[:, None], -(NA_WIN_COLS - 1), NA_WIN_COLS - 1) + (NA_WIN_COLS - 1)
    return rel_bias[:, :, dc]


def _pool_kernel(u_ref, w_ref, s_ref, *rest, seq):
    o_ref = rest[-1]
    pos = lax.broadcasted_iota(jnp.int32, (seq, POOL_GROUP), 0)
    for g, win in enumerate(POOL_WINDOWS):
        sl = slice(g * POOL_GROUP, (g + 1) * POOL_GROUP)
        u = u_ref[:, sl]
        acc = jnp.zeros_like(u)
        for d in range(-(win // 2), win - win // 2):
            shifted = u if d == 0 else pltpu.roll(u, (-d) % seq, 0)
            valid = (pos + d >= 0) & (pos + d < seq)
            acc = acc + jnp.where(valid, shifted, 0.0)
        lo = jnp.clip(pos - win // 2, 0, seq)
        hi = jnp.clip(pos + win - win // 2, 0, seq)
        pooled = acc / (hi - lo).astype(F32) - u
        y = jnp.dot(pooled.astype(BF16), w_ref[g].astype(BF16), preferred_element_type=F32)
        o_ref[:, sl] = (y * s_ref[:, sl]).astype(o_ref.dtype)


def _pool(proj, row0, n_seq, seq, pool_w, pool_scale, prev=None):
    in_specs = [pl.BlockSpec((seq, POOL_WIDTH), lambda b: (row0 // seq + b, COL_POOL // POOL_WIDTH)),
                pl.BlockSpec(pool_w.shape, lambda b: (0, 0, 0)),
                pl.BlockSpec((1, POOL_WIDTH), lambda b: (0, 0))]
    args = [proj, pool_w, pool_scale]
    aliases = {}
    if prev is not None:
        in_specs.append(_ANY)
        args.append(prev)
        aliases = {3: 0}
    return pl.pallas_call(
        functools.partial(_pool_kernel, seq=seq),
        out_shape=jax.ShapeDtypeStruct((proj.shape[0], POOL_WIDTH), BF16),
        grid=(n_seq,),
        in_specs=in_specs,
        out_specs=pl.BlockSpec((seq, POOL_WIDTH), lambda b: (row0 // seq + b, 0)),
        input_output_aliases=aliases,
        compiler_params=_params(("parallel",)),
        name="pool",
    )(*args)


def _log_sigmoid(x):
    return jnp.minimum(x, 0.0) - jnp.log1p(jnp.exp(-jnp.abs(x)))


def _mlstm_chunk(q, k, v, i_col, f_col, c_prev, n_prev, m_prev, eye, vis, vis_t):
    f_row = jnp.sum(jnp.where(eye, f_col, 0.0), axis=0, keepdims=True)
    i_row = jnp.sum(jnp.where(eye, i_col, 0.0), axis=0, keepdims=True)
    b_col = jnp.sum(jnp.where(vis, f_row, 0.0), axis=1, keepdims=True)
    b_row = jnp.sum(jnp.where(vis_t, f_col, 0.0), axis=0, keepdims=True)
    total = jnp.sum(f_col, axis=0, keepdims=True)
    dmat = jnp.where(vis, b_col - b_row + i_row, -jnp.inf)
    inter = b_col + m_prev
    m_t = jnp.maximum(inter, jnp.max(dmat, axis=1, keepdims=True))
    w_inter = jnp.exp(inter - m_t)
    ks = k * (ML_HEAD_DIM ** -0.5)
    qb = q.astype(BF16)
    vb = v.astype(BF16)
    qk = _dot_nt(qb, ks.astype(BF16)) * jnp.exp(dmat - m_t)
    num = (jnp.dot(qk.astype(BF16), vb, preferred_element_type=F32)
           + w_inter * jnp.dot(qb, c_prev.astype(BF16), preferred_element_type=F32))
    den = jnp.sum(qk, axis=1, keepdims=True) + w_inter * jnp.sum(q * n_prev, axis=1, keepdims=True)
    h = num / jnp.maximum(jnp.abs(den), jnp.exp(-m_t))
    g = total - b_col + i_col
    m_new = jnp.maximum(total + m_prev, jnp.max(g, axis=0, keepdims=True))
    ws = jnp.exp(g - m_new)
    decay = jnp.exp(total + m_prev - m_new)
    kw = ks * ws
    c_new = decay * c_prev + lax.dot_general(kw.astype(BF16), vb, (((0,), (0,)), ((), ())),
                                             preferred_element_type=F32)
    n_new = decay * n_prev + jnp.sum(kw, axis=0, keepdims=True)
    return h, c_new, n_new, m_new


def _mlstm_kernel(*refs, has_init, n_prev, emit_state):
    qf, kf, vf, gf, qb, kb, vb, gb = refs[:8]
    pos = 8
    if has_init:
        c0_ref, n0_ref, m0_ref = refs[pos:pos + 3]
        pos += 3
    pos += n_prev
    hf_ref, hb_ref = refs[pos:pos + 2]
    pos += 2
    if emit_state:
        co_ref, no_ref, mo_ref = refs[pos:pos + 3]
        pos += 3
    c_s, n_s, m_s = refs[pos:pos + 3]
    c = pl.program_id(1)

    @pl.when(c == 0)
    def _():
        if has_init:
            c_s[...] = c0_ref[...]
            n_s[...] = n0_ref[...]
            m_s[...] = m0_ref[...]
        else:
            c_s[...] = jnp.zeros_like(c_s)
            n_s[...] = jnp.zeros_like(n_s)
            m_s[...] = jnp.zeros_like(m_s)

    t_idx = lax.broadcasted_iota(jnp.int32, (ML_CHUNK, ML_CHUNK), 0)
    s_idx = lax.broadcasted_iota(jnp.int32, (ML_CHUNK, ML_CHUNK), 1)
    eye = t_idx == s_idx
    lower = s_idx <= t_idx
    upper = s_idx >= t_idx
    for d, (q_ref, k_ref, v_ref, g_ref, h_ref) in enumerate(((qf, kf, vf, gf, hf_ref), (qb, kb, vb, gb, hb_ref))):
        vis, vis_t = (lower, upper) if d == 0 else (upper, lower)
        gates = g_ref[...]
        log_f = _log_sigmoid(gates)
        for h in range(ML_HEADS):
            sl = slice(h * ML_HEAD_DIM, (h + 1) * ML_HEAD_DIM)
            ci = 2 * d * ML_HEADS + h
            cf = (2 * d + 1) * ML_HEADS + h
            out, c_new, n_new, m_new = _mlstm_chunk(
                q_ref[:, sl], k_ref[:, sl], v_ref[:, sl], gates[:, ci:ci + 1], log_f[:, cf:cf + 1],
                c_s[d, h], n_s[d, h], m_s[d, h], eye, vis, vis_t)
            h_ref[:, sl] = out
            c_s[d, h] = c_new
            n_s[d, h] = n_new
            m_s[d, h] = m_new

    if emit_state:
        @pl.when(c == pl.num_programs(1) - 1)
        def _():
            co_ref[...] = c_s[...]
            no_ref[...] = n_s[...]
            mo_ref[...] = m_s[...]


def _mlstm(proj, gates, row0, n_seq, seq, layer, init=None, prev_h=None, emit_state=False, prev_state=None):
    nc = seq // ML_CHUNK
    dh = ML_HEAD_DIM
    rb = row0 // ML_CHUNK
    m_tot = proj.shape[0]

    def chunk_row(b, c, rev):
        return rb + b * nc + ((nc - 1 - c) if rev else c)

    def tok(col, rev):
        return pl.BlockSpec((ML_CHUNK, ML_WIDTH), lambda b, c: (chunk_row(b, c, rev), col // ML_WIDTH))

    def gat(rev):
        return pl.BlockSpec((ML_CHUNK, GATE_PAD), lambda b, c: (chunk_row(b, c, rev), 0))

    def hout(rev):
        return pl.BlockSpec((ML_CHUNK, ML_WIDTH), lambda b, c: (chunk_row(b, c, rev), 0))

    in_specs = [tok(COL_ML_Q, False), tok(COL_ML_K, False), tok(COL_ML_V, False), gat(False),
                tok(COL_ML_Q, True), tok(COL_ML_K, True), tok(COL_ML_V, True), gat(True)]
    args = [proj, proj, proj, gates, proj, proj, proj, gates]
    state_idx = lambda b, c: (b, layer, 0, 0, 0, 0)
    if init is not None:
        c0, n0, m0 = init
        in_specs += [_resident((None, None, 2, ML_HEADS, dh, dh), state_idx),
                     _resident((None, None, 2, ML_HEADS, 1, dh), state_idx),
                     _resident((None, None, 2, ML_HEADS, 1, 1), state_idx)]
        args += [c0, n0, m0]
    aliases = {}
    prevs = list(prev_h or ()) + list(prev_state or ())
    out_base = 0 if prev_h else 2
    for i, p in enumerate(prevs):
        aliases[len(args)] = out_base + i
        in_specs.append(_ANY)
        args.append(p)
    out_shape = [jax.ShapeDtypeStruct((m_tot, ML_WIDTH), F32)] * 2
    out_specs = [hout(False), hout(True)]
    if emit_state:
        out_shape += [jax.ShapeDtypeStruct((n_seq, DEPTH, 2, ML_HEADS, dh, dh), F32),
                      jax.ShapeDtypeStruct((n_seq, DEPTH, 2, ML_HEADS, 1, dh), F32),
                      jax.ShapeDtypeStruct((n_seq, DEPTH, 2, ML_HEADS, 1, 1), F32)]
        out_specs += [pl.BlockSpec((None, None, 2, ML_HEADS, dh, dh), state_idx),
                      pl.BlockSpec((None, None, 2, ML_HEADS, 1, dh), state_idx),
                      pl.BlockSpec((None, None, 2, ML_HEADS, 1, 1), state_idx)]
    return pl.pallas_call(
        functools.partial(_mlstm_kernel, has_init=init is not None, n_prev=len(prevs), emit_state=emit_state),
        out_shape=out_shape,
        grid=(n_seq, nc),
        in_specs=in_specs,
        out_specs=out_specs,
        input_output_aliases=aliases,
        scratch_shapes=[pltpu.VMEM((2, ML_HEADS, dh, dh), F32), pltpu.VMEM((2, ML_HEADS, 1, dh), F32),
                        pltpu.VMEM((2, ML_HEADS, 1, 1), F32)],
        compiler_params=_params(("parallel", "arbitrary")),
        name="mlstm",
    )(*args)


def _ml_post_kernel(hf_ref, hb_ref, o_ref, g_ref, out_ref):
    for h in range(ML_HEADS):
        sl = slice(h * ML_HEAD_DIM, (h + 1) * ML_HEAD_DIM)
        x = hf_ref[:, sl] + hb_ref[:, sl]
        x = x * lax.rsqrt(jnp.mean(x * x, axis=-1, keepdims=True) + EPS)
        x = x * g_ref[:, sl]
        out_ref[:, sl] = (jax.nn.sigmoid(o_ref[:, sl]) * x).astype(out_ref.dtype)


def _ml_post(hf, hb, proj, norm_g, bm=256):
    m = hf.shape[0]
    row = lambda i: (i, 0)
    return pl.pallas_call(
        _ml_post_kernel,
        out_shape=jax.ShapeDtypeStruct((m, ML_WIDTH), BF16),
        grid=(m // bm,),
        in_specs=[pl.BlockSpec((bm, ML_WIDTH), row), pl.BlockSpec((bm, ML_WIDTH), row),
                  pl.BlockSpec((bm, ML_WIDTH), lambda i: (i, COL_ML_O // ML_WIDTH)),
                  pl.BlockSpec((1, ML_WIDTH), lambda i: (0, 0))],
        out_specs=pl.BlockSpec((bm, ML_WIDTH), row),
        compiler_params=_params(("parallel",)),
        name="ml_post",
    )(hf, hb, proj, norm_g)


def kernel(x_prompt, x_sample, c, cache_na_k, cache_na_v, state_mlstm_C, state_mlstm_n, state_mlstm_m, c_ctx, w_ada, b_ada, norm_g, ffn_w_in, ffn_w_out, w_in, na_rel_bias, pool_w, pool_scale, ml_gate_bias, ml_norm_g, w_branch, w_out, final_norm_g):
    batch, seq, d = x_prompt.shape
    dec_batch, dec_seq, _ = x_sample.shape
    past = cache_na_k.shape[2]
    assert d == D_MODEL and dec_batch + 1 <= MOD_ROWS
    m_ctx = batch * seq
    m_lat = dec_batch * dec_seq
    groups = _Groups(m_ctx, dec_seq)

    x = jnp.concatenate([x_prompt.reshape(m_ctx, d), x_sample.reshape(m_lat, d)], axis=0)
    cond = jnp.concatenate([c_ctx[None], c, jnp.zeros((MOD_ROWS - 1 - dec_batch, d), F32)], axis=0)
    mod_all = _modulation(cond, w_ada, b_ada.reshape(DEPTH, 1, N_MOD * d))
    mod_all = mod_all.reshape(DEPTH, MOD_ROWS, N_MOD, d)

    ctx_k = cache_na_k.reshape(dec_batch, DEPTH, past, NA_WIDTH)
    ctx_v = cache_na_v.reshape(dec_batch, DEPTH, past, NA_WIDTH)
    init = (state_mlstm_C,
            state_mlstm_n.reshape(dec_batch, DEPTH, 2, ML_HEADS, 1, ML_HEAD_DIM),
            state_mlstm_m.reshape(dec_batch, DEPTH, 2, ML_HEADS, 1, 1))
    w_ffn_out = ffn_w_out.astype(BF16)

    ks_out, vs_out = [], []
    states = None
    for l in range(DEPTH):
        mod = mod_all[l]
        w_merge = w_in[l, :, MAIN_COLS + GATE_COLS:].astype(BF16)
        b_gate = jnp.pad(ml_gate_bias[l], (0, GATE_PAD - GATE_COLS)).reshape(1, GATE_PAD)

        def ffn(x, j, shift_row):
            h = _norm_mod(x, norm_g[l, 2 * j].reshape(1, d), mod, groups, shift_row)
            act = _mm_swiglu(h, ffn_w_in, l, j)
            return _mm_resid(act, w_ffn_out, (l, j), x, mod, groups, shift_row + 2, MACARON_W, bm=1024)

        x = ffn(x, 0, 0)

        h = _norm_mod(x, norm_g[l, 1].reshape(1, d), mod, groups, 3)
        proj = _mm_proj(h, w_in, l)
        gates = _mm_gates(h, w_in, l, b_gate)

        na_out = _ctx_attention(proj, batch, seq)
        na_out = _na_attention(proj, m_ctx, dec_batch, dec_seq, ctx_k, ctx_v, l, _rel_bias_table(na_rel_bias[l]),
                               na_out)
        ps = pool_scale[l].reshape(1, POOL_WIDTH)
        pool_out = _pool(proj, 0, batch, seq, pool_w[l], ps)
        pool_out = _pool(proj, m_ctx, dec_batch, dec_seq, pool_w[l], ps, prev=pool_out)
        hf, hb, *states = _mlstm(proj, gates, 0, batch, seq, l, emit_state=True, prev_state=states)
        hf, hb = _mlstm(proj, gates, m_ctx, dec_batch, dec_seq, l, init=init, prev_h=(hf, hb))
        ml_out = _ml_post(hf, hb, proj, ml_norm_g[l].reshape(1, ML_WIDTH))

        merged = _mm_merge(h, na_out, pool_out, ml_out, w_merge, w_branch, l)
        x = _mm_resid(merged, w_out, (l,), x, mod, groups, 5, 1.0, bm=2048)

        x = ffn(x, 1, 6)

        ks_out.append(proj[:m_ctx, COL_NA_K:COL_NA_K + NA_WIDTH].reshape(batch, seq, NA_HEADS, NA_HEAD_DIM))
        vs_out.append(proj[:m_ctx, COL_NA_V:COL_NA_V + NA_WIDTH].reshape(batch, seq, NA_HEADS, NA_HEAD_DIM))

    g_fin = final_norm_g.reshape(1, d)
    y_prompt = _final_norm(x, 0, m_ctx, g_fin).reshape(batch, seq, d)
    y_sample = _final_norm(x, m_ctx, m_lat, g_fin).reshape(dec_batch, dec_seq, d)
    c_fin, n_fin, m_fin = states
    return (y_prompt, y_sample, jnp.stack(ks_out, axis=1), jnp.stack(vs_out, axis=1),
            c_fin, n_fin.reshape(batch, DEPTH, 2, ML_HEADS, ML_HEAD_DIM), m_fin.reshape(batch, DEPTH, 2, ML_HEADS))
```

```python
import functools

import jax
import jax.numpy as jnp
import numpy as np
from jax import lax
from jax.experimental import pallas as pl
from jax.experimental.pallas import tpu as pltpu

F32 = jnp.float32
BF16 = jnp.bfloat16

D_MODEL = 4096
DEPTH = 2
GRID_W = 64
NA_HEADS = 8
NA_WIDTH = D_MODEL // 4
NA_HEAD_DIM = NA_WIDTH // NA_HEADS
NA_WIN_ROWS = 8
NA_WIN_COLS = 16
POOL_WINDOWS = (2, 4, 8, 16)
POOL_WIDTH = D_MODEL // 4
POOL_GROUP = POOL_WIDTH // 4
ML_HEADS = 8
ML_WIDTH = D_MODEL // 2
ML_HEAD_DIM = ML_WIDTH // ML_HEADS
ML_CHUNK = 256
D_FF = 256 * ((8 * D_MODEL // 3 + 255) // 256)
N_MOD = 9
MACARON_W = 0.5
EPS = 1e-6
MASK_VALUE = -1e30

COL_NA_Q = 0
COL_NA_K = NA_WIDTH
COL_NA_V = 2 * NA_WIDTH
COL_POOL = 3 * NA_WIDTH
COL_ML_Q = COL_POOL + POOL_WIDTH
COL_ML_K = COL_ML_Q + ML_WIDTH
COL_ML_V = COL_ML_K + ML_WIDTH
COL_ML_O = COL_ML_V + ML_WIDTH
MAIN_COLS = COL_ML_O + ML_WIDTH
GATE_COLS = 4 * ML_HEADS
GATE_PAD = 128

MOD_ROWS = 8
VMEM_LIMIT = 56 * 1024 * 1024


def _params(sem, vmem=VMEM_LIMIT):
    return pltpu.CompilerParams(dimension_semantics=sem, vmem_limit_bytes=vmem)


def _resident(block_shape, index_map):
    return pl.BlockSpec(block_shape, index_map, pipeline_mode=pl.Buffered(1))


_ANY = pl.BlockSpec(memory_space=pl.ANY)


def _modulation_kernel(c_ref, w_ref, b_ref, o_ref):
    c = c_ref[...]
    s = (c * jax.nn.sigmoid(c)).astype(BF16)
    o_ref[...] = jnp.dot(s, w_ref[...].astype(BF16), preferred_element_type=F32) + b_ref[...]


def _modulation(cond, w_ada, b_ada, bn=512):
    n = w_ada.shape[-1]
    return pl.pallas_call(
        _modulation_kernel,
        out_shape=jax.ShapeDtypeStruct((DEPTH, MOD_ROWS, n), F32),
        grid=(DEPTH, n // bn),
        in_specs=[
            pl.BlockSpec((MOD_ROWS, D_MODEL), lambda l, j: (0, 0)),
            pl.BlockSpec((None, D_MODEL, bn), lambda l, j: (l, 0, j)),
            pl.BlockSpec((None, 1, bn), lambda l, j: (l, 0, j)),
        ],
        out_specs=pl.BlockSpec((None, MOD_ROWS, bn), lambda l, j: (l, 0, j)),
        compiler_params=_params(("parallel", "parallel")),
        name="modulation",
    )(cond, w_ada, b_ada)


def _norm_mod_kernel(x_ref, g_ref, mod_ref, o_ref, *, shift_row):
    x = x_ref[...]
    y = x * lax.rsqrt(jnp.mean(x * x, axis=-1, keepdims=True) + EPS)
    y = y * g_ref[...]
    shift = mod_ref[shift_row:shift_row + 1, :]
    scale = mod_ref[shift_row + 1:shift_row + 2, :]
    o_ref[...] = (y * (1 + scale) + shift).astype(o_ref.dtype)


def _final_norm_kernel(x_ref, g_ref, o_ref):
    x = x_ref[...]
    y = x * lax.rsqrt(jnp.mean(x * x, axis=-1, keepdims=True) + EPS)
    o_ref[...] = y * g_ref[...]


class _Groups:
    def __init__(self, m_ctx, dec_seq):
        self.m_ctx = m_ctx
        self.dec_seq = dec_seq

    def of_block(self, i, bm):
        assert self.m_ctx % bm == 0 and self.dec_seq % bm == 0
        return jnp.maximum((i * bm) // self.dec_seq - (self.m_ctx // self.dec_seq - 1), 0)


def _norm_mod(x, g, mod, groups, shift_row, bm=256):
    m, d = x.shape
    return pl.pallas_call(
        functools.partial(_norm_mod_kernel, shift_row=shift_row),
        out_shape=jax.ShapeDtypeStruct((m, d), BF16),
        grid=(m // bm,),
        in_specs=[
            pl.BlockSpec((bm, d), lambda i: (i, 0)),
            pl.BlockSpec((1, d), lambda i: (0, 0)),
            pl.BlockSpec((None, N_MOD, d), lambda i: (groups.of_block(i, bm), 0, 0)),
        ],
        out_specs=pl.BlockSpec((bm, d), lambda i: (i, 0)),
        compiler_params=_params(("parallel",)),
        name="norm_mod",
    )(x, g, mod)


def _final_norm(x, row0, rows, g, bm=256):
    d = x.shape[1]
    return pl.pallas_call(
        _final_norm_kernel,
        out_shape=jax.ShapeDtypeStruct((rows, d), F32),
        grid=(rows // bm,),
        in_specs=[pl.BlockSpec((bm, d), lambda i: (row0 // bm + i, 0)), pl.BlockSpec((1, d), lambda i: (0, 0))],
        out_specs=pl.BlockSpec((bm, d), lambda i: (i, 0)),
        compiler_params=_params(("parallel",)),
        name="final_norm",
    )(x, g)


def _dot_nt(a, b):
    return lax.dot_general(a, b, (((1,), (1,)), ((), ())), preferred_element_type=F32)


def _mm_plain_kernel(h_ref, wt_ref, o_ref):
    o_ref[...] = _dot_nt(h_ref[...], wt_ref[...].astype(BF16))


def _mm_bias_kernel(h_ref, wt_ref, b_ref, o_ref):
    o_ref[...] = _dot_nt(h_ref[...], wt_ref[...].astype(BF16)) + b_ref[...]


def _mm_swiglu_kernel(h_ref, wa_ref, wb_ref, wo_ref, o_ref, wo_bf16_ref):
    wo_bf16_ref[...] = wo_ref[...].astype(BF16)
    h = h_ref[...]
    a = jnp.dot(h, wa_ref[...].astype(BF16), preferred_element_type=F32)
    b = jnp.dot(h, wb_ref[...].astype(BF16), preferred_element_type=F32)
    o_ref[...] = ((a * jax.nn.sigmoid(a)) * b).astype(o_ref.dtype)


def _mm_resid_kernel(a_ref, w_ref, x_ref, mod_ref, o_ref, *, gate_row, coef):
    y = jnp.dot(a_ref[...], w_ref[...].astype(BF16), preferred_element_type=F32)
    gate = mod_ref[gate_row:gate_row + 1, :]
    o_ref[...] = x_ref[...] + (coef * gate) * y


def _mm_merge_kernel(h_ref, na_ref, po_ref, ml_ref, wg_na, wg_po, wg_ml, wb_na, wb_po, wb_ml, o_ref):
    h = h_ref[...]

    def branch(x_ref, wg_ref, wb_ref):
        g = _dot_nt(h, wg_ref[...])
        y = jnp.dot(x_ref[...], wb_ref[...].astype(BF16), preferred_element_type=F32)
        return jax.nn.sigmoid(g) * y

    o = branch(na_ref, wg_na, wb_na) + branch(po_ref, wg_po, wb_po) + branch(ml_ref, wg_ml, wb_ml)
    o_ref[...] = o.astype(o_ref.dtype)


def _mm_proj(h, w_in_t, layer, bm=2048, bn=256):
    m, k = h.shape
    return pl.pallas_call(
        _mm_plain_kernel,
        out_shape=jax.ShapeDtypeStruct((m, MAIN_COLS), F32),
        grid=(m // bm, MAIN_COLS // bn),
        in_specs=[_resident((bm, k), lambda i, j: (i, 0)),
                  pl.BlockSpec((None, bn, k), lambda i, j: (layer, j, 0))],
        out_specs=pl.BlockSpec((bm, bn), lambda i, j: (i, j)),
        compiler_params=_params(("parallel", "arbitrary")),
        name="mm_proj",
    )(h, w_in_t)


def _mm_gates(h, w_in_t, layer, b, bm=1024):
    m, k = h.shape
    assert MAIN_COLS % GATE_PAD == 0
    return pl.pallas_call(
        _mm_bias_kernel,
        out_shape=jax.ShapeDtypeStruct((m, GATE_PAD), F32),
        grid=(m // bm,),
        in_specs=[pl.BlockSpec((bm, k), lambda i: (i, 0)),
                  pl.BlockSpec((None, GATE_PAD, k), lambda i: (layer, MAIN_COLS // GATE_PAD, 0)),
                  pl.BlockSpec((1, GATE_PAD), lambda i: (0, 0))],
        out_specs=pl.BlockSpec((bm, GATE_PAD), lambda i: (i, 0)),
        compiler_params=_params(("parallel",)),
        name="mm_gates",
    )(h, w_in_t, b)


def _mm_swiglu(h, ffn_w_in, ffn_w_out, layer, j, bm=2048, bn=256):
    m, k = h.shape
    nb = D_FF // bn
    steps = (m // bm) * nb
    slab = D_FF // steps
    assert slab * steps == D_FF and slab % 16 == 0
    d_out = ffn_w_out.shape[-1]
    return pl.pallas_call(
        _mm_swiglu_kernel,
        out_shape=[jax.ShapeDtypeStruct((m, D_FF), BF16), jax.ShapeDtypeStruct((D_FF, d_out), BF16)],
        grid=(m // bm, nb),
        in_specs=[_resident((bm, k), lambda i, n: (i, 0)),
                  pl.BlockSpec((None, None, k, bn), lambda i, n: (layer, j, 0, n)),
                  pl.BlockSpec((None, None, k, bn), lambda i, n: (layer, j, 0, n + nb)),
                  pl.BlockSpec((None, None, slab, d_out), lambda i, n: (layer, j, i * nb + n, 0))],
        out_specs=[pl.BlockSpec((bm, bn), lambda i, n: (i, n)),
                   pl.BlockSpec((slab, d_out), lambda i, n: (i * nb + n, 0))],
        compiler_params=_params(("parallel", "arbitrary")),
        name="mm_swiglu",
    )(h, ffn_w_in, ffn_w_in, ffn_w_out)


def _mm_resid(a, w, w_index, x, mod, groups, gate_row, coef, bm, bn=256):
    m, k = a.shape
    n = w.shape[-1]
    lead = (None,) * len(w_index)
    return pl.pallas_call(
        functools.partial(_mm_resid_kernel, gate_row=gate_row, coef=coef),
        out_shape=jax.ShapeDtypeStruct((m, n), F32),
        grid=(m // bm, n // bn),
        in_specs=[_resident((bm, k), lambda i, j: (i, 0)),
                  pl.BlockSpec(lead + (k, bn), lambda i, j: w_index + (0, j)),
                  pl.BlockSpec((bm, bn), lambda i, j: (i, j)),
                  pl.BlockSpec((None, N_MOD, bn), lambda i, j: (groups.of_block(i, bm), 0, j))],
        out_specs=pl.BlockSpec((bm, bn), lambda i, j: (i, j)),
        compiler_params=_params(("parallel", "arbitrary")),
        name="mm_resid",
    )(a, w, x, mod)


def _mm_merge(h, na, po, ml, wg, w_branch, layer, bm=1024, bn=256):
    m, d = h.shape
    nb = d // bn
    row = lambda i, j: (i, 0)
    return pl.pallas_call(
        _mm_merge_kernel,
        out_shape=jax.ShapeDtypeStruct((m, d), BF16),
        grid=(m // bm, nb),
        in_specs=[_resident((bm, d), row),
                  _resident((bm, NA_WIDTH), row),
                  _resident((bm, POOL_WIDTH), row),
                  _resident((bm, ML_WIDTH), row),
                  pl.BlockSpec((bn, d), lambda i, j: (j, 0)),
                  pl.BlockSpec((bn, d), lambda i, j: (j + nb, 0)),
                  pl.BlockSpec((bn, d), lambda i, j: (j + 2 * nb, 0)),
                  pl.BlockSpec((None, NA_WIDTH, bn), lambda i, j: (layer, 0, j)),
                  pl.BlockSpec((None, POOL_WIDTH, bn), lambda i, j: (layer, NA_WIDTH // POOL_WIDTH, j)),
                  pl.BlockSpec((None, ML_WIDTH, bn), lambda i, j: (layer, (NA_WIDTH + POOL_WIDTH) // ML_WIDTH, j))],
        out_specs=pl.BlockSpec((bm, bn), lambda i, j: (i, j)),
        compiler_params=_params(("parallel", "arbitrary")),
        name="mm_merge",
    )(h, na, po, ml, wg, wg, wg, w_branch, w_branch, w_branch)


def _softmax_rows(parts):
    m = parts[0].max(axis=-1, keepdims=True)
    for s in parts[1:]:
        m = jnp.maximum(m, s.max(axis=-1, keepdims=True))
    es = [jnp.exp(s - m) for s in parts]
    den = es[0].sum(axis=-1, keepdims=True)
    for e in es[1:]:
        den = den + e.sum(axis=-1, keepdims=True)
    return [e / den for e in es]


def _ctx_attn_kernel(q_ref, k_ref, v_ref, o_ref):
    scale = NA_HEAD_DIM ** -0.5
    for h in range(NA_HEADS):
        sl = slice(h * NA_HEAD_DIM, (h + 1) * NA_HEAD_DIM)
        q = q_ref[:, sl].astype(BF16)
        k = k_ref[:, sl].astype(BF16)
        v = v_ref[:, sl].astype(BF16)
        (p,) = _softmax_rows([_dot_nt(q, k) * scale])
        o_ref[:, sl] = jnp.dot(p.astype(BF16), v, preferred_element_type=F32).astype(o_ref.dtype)


def _ctx_attention(proj, n_seq, seq):
    blk = lambda c: pl.BlockSpec((seq, NA_WIDTH), lambda b: (b, c // NA_WIDTH))
    return pl.pallas_call(
        _ctx_attn_kernel,
        out_shape=jax.ShapeDtypeStruct((proj.shape[0], NA_WIDTH), BF16),
        grid=(n_seq,),
        in_specs=[blk(COL_NA_Q), blk(COL_NA_K), blk(COL_NA_V)],
        out_specs=pl.BlockSpec((seq, NA_WIDTH), lambda b: (b, 0)),
        compiler_params=_params(("parallel",)),
        name="ctx_attention",
    )(proj, proj, proj)


def _na_attn_kernel(q_ref, k_ref, v_ref, ck_ref, cv_ref, bias_ref, prev_ref, o_ref, *, rows):
    del prev_ref
    r = pl.program_id(1)
    kr = min(NA_WIN_ROWS, rows)
    n_loc = kr * GRID_W
    start = jnp.clip(r - kr // 2, 0, rows - kr)
    k0 = pl.multiple_of(start * GRID_W, GRID_W)
    dr0 = start - r + (NA_WIN_ROWS - 1)
    scale = NA_HEAD_DIM ** -0.5
    wq = lax.broadcasted_iota(jnp.int32, (GRID_W, n_loc), 0)
    xk = lax.broadcasted_iota(jnp.int32, (GRID_W, n_loc), 1) % GRID_W
    cs = jnp.clip(wq - NA_WIN_COLS // 2, 0, GRID_W - NA_WIN_COLS)
    col_mask = (xk >= cs) & (xk < cs + NA_WIN_COLS)
    for h in range(NA_HEADS):
        sl = slice(h * NA_HEAD_DIM, (h + 1) * NA_HEAD_DIM)
        q = q_ref[:, sl].astype(BF16)
        kl = k_ref[pl.ds(k0, n_loc), sl].astype(BF16)
        vl = v_ref[pl.ds(k0, n_loc), sl].astype(BF16)
        bias = jnp.concatenate([bias_ref[h, dr0 + i] for i in range(kr)], axis=-1)
        s_loc = _dot_nt(q, kl) * scale + bias
        s_loc = jnp.where(col_mask, s_loc, MASK_VALUE)
        s_ctx = _dot_nt(q, ck_ref[:, sl].astype(BF16)) * scale
        p_loc, p_ctx = _softmax_rows([s_loc, s_ctx])
        out = (jnp.dot(p_loc.astype(BF16), vl, preferred_element_type=F32)
               + jnp.dot(p_ctx.astype(BF16), cv_ref[:, sl].astype(BF16), preferred_element_type=F32))
        o_ref[:, sl] = out.astype(o_ref.dtype)


def _na_attention(proj, row0, n_seq, seq, ctx_k, ctx_v, layer, bias_tab, prev):
    rows = seq // GRID_W
    past = ctx_k.shape[2]
    rb = row0 // GRID_W
    q_spec = pl.BlockSpec((GRID_W, NA_WIDTH), lambda b, r: (rb + b * rows + r, COL_NA_Q // NA_WIDTH))
    kv = lambda c: _resident((seq, NA_WIDTH), lambda b, r: (row0 // seq + b, c // NA_WIDTH))
    ctx = _resident((None, None, past, NA_WIDTH), lambda b, r: (b, layer, 0, 0))
    return pl.pallas_call(
        functools.partial(_na_attn_kernel, rows=rows),
        out_shape=jax.ShapeDtypeStruct(prev.shape, prev.dtype),
        grid=(n_seq, rows),
        in_specs=[q_spec, kv(COL_NA_K), kv(COL_NA_V), ctx, ctx,
                  _resident(bias_tab.shape, lambda b, r: (0, 0, 0, 0)), _ANY],
        out_specs=pl.BlockSpec((GRID_W, NA_WIDTH), lambda b, r: (rb + b * rows + r, 0)),
        input_output_aliases={6: 0},
        compiler_params=_params(("parallel", "arbitrary")),
        name="na_attention",
    )(proj, proj, proj, ctx_k, ctx_v, bias_tab, prev)


def _rel_bias_table(rel_bias):
    cq = np.arange(GRID_W)
    dc = np.clip(cq[None, :] - cq[:, None], -(NA_WIN_COLS - 1), NA_WIN_COLS - 1) + (NA_WIN_COLS - 1)
    return rel_bias[:, :, dc]


def _pool_kernel(u_ref, w_ref, s_ref, *rest, seq):
    o_ref = rest[-1]
    pos = lax.broadcasted_iota(jnp.int32, (seq, POOL_GROUP), 0)
    for g, win in enumerate(POOL_WINDOWS):
        sl = slice(g * POOL_GROUP, (g + 1) * POOL_GROUP)
        u = u_ref[:, sl]
        acc = jnp.zeros_like(u)
        for d in range(-(win // 2), win - win // 2):
            shifted = u if d == 0 else pltpu.roll(u, (-d) % seq, 0)
            valid = (pos + d >= 0) & (pos + d < seq)
            acc = acc + jnp.where(valid, shifted, 0.0)
        lo = jnp.clip(pos - win // 2, 0, seq)
        hi = jnp.clip(pos + win - win // 2, 0, seq)
        pooled = acc / (hi - lo).astype(F32) - u
        y = jnp.dot(pooled.astype(BF16), w_ref[g].astype(BF16), preferred_element_type=F32)
        o_ref[:, sl] = (y * s_ref[:, sl]).astype(o_ref.dtype)


def _pool(proj, row0, n_seq, seq, pool_w, pool_scale, prev=None):
    in_specs = [pl.BlockSpec((seq, POOL_WIDTH), lambda b: (row0 // seq + b, COL_POOL // POOL_WIDTH)),
                pl.BlockSpec(pool_w.shape, lambda b: (0, 0, 0)),
                pl.BlockSpec((1, POOL_WIDTH), lambda b: (0, 0))]
    args = [proj, pool_w, pool_scale]
    aliases = {}
    if prev is not None:
        in_specs.append(_ANY)
        args.append(prev)
        aliases = {3: 0}
    return pl.pallas_call(
        functools.partial(_pool_kernel, seq=seq),
        out_shape=jax.ShapeDtypeStruct((proj.shape[0], POOL_WIDTH), BF16),
        grid=(n_seq,),
        in_specs=in_specs,
        out_specs=pl.BlockSpec((seq, POOL_WIDTH), lambda b: (row0 // seq + b, 0)),
        input_output_aliases=aliases,
        compiler_params=_params(("parallel",)),
        name="pool",
    )(*args)


def _log_sigmoid(x):
    return jnp.minimum(x, 0.0) - jnp.log1p(jnp.exp(-jnp.abs(x)))


def _mlstm_chunk(q, k, v, i_col, f_col, c_prev, n_prev, m_prev, eye, vis, vis_t):
    f_row = jnp.sum(jnp.where(eye, f_col, 0.0), axis=0, keepdims=True)
    i_row = jnp.sum(jnp.where(eye, i_col, 0.0), axis=0, keepdims=True)
    b_col = jnp.sum(jnp.where(vis, f_row, 0.0), axis=1, keepdims=True)
    b_row = jnp.sum(jnp.where(vis_t, f_col, 0.0), axis=0, keepdims=True)
    total = jnp.sum(f_col, axis=0, keepdims=True)
    dmat = jnp.where(vis, b_col - b_row + i_row, -jnp.inf)
    inter = b_col + m_prev
    m_t = jnp.maximum(inter, jnp.max(dmat, axis=1, keepdims=True))
    w_inter = jnp.exp(inter - m_t)
    ks = k * (ML_HEAD_DIM ** -0.5)
    qb = q.astype(BF16)
    vb = v.astype(BF16)
    qk = _dot_nt(qb, ks.astype(BF16)) * jnp.exp(dmat - m_t)
    num = (jnp.dot(qk.astype(BF16), vb, preferred_element_type=F32)
           + w_inter * jnp.dot(qb, c_prev.astype(BF16), preferred_element_type=F32))
    den = jnp.sum(qk, axis=1, keepdims=True) + w_inter * jnp.sum(q * n_prev, axis=1, keepdims=True)
    h = num / jnp.maximum(jnp.abs(den), jnp.exp(-m_t))
    g = total - b_col + i_col
    m_new = jnp.maximum(total + m_prev, jnp.max(g, axis=0, keepdims=True))
    ws = jnp.exp(g - m_new)
    decay = jnp.exp(total + m_prev - m_new)
    kw = ks * ws
    c_new = decay * c_prev + lax.dot_general(kw.astype(BF16), vb, (((0,), (0,)), ((), ())),
                                             preferred_element_type=F32)
    n_new = decay * n_prev + jnp.sum(kw, axis=0, keepdims=True)
    return h, c_new, n_new, m_new


def _mlstm_kernel(*refs, has_init, n_prev, emit_state):
    qf, kf, vf, gf, qb, kb, vb, gb = refs[:8]
    pos = 8
    if has_init:
        c0_ref, n0_ref, m0_ref = refs[pos:pos + 3]
        pos += 3
    pos += n_prev
    hf_ref, hb_ref = refs[pos:pos + 2]
    pos += 2
    if emit_state:
        co_ref, no_ref, mo_ref = refs[pos:pos + 3]
        pos += 3
    c_s, n_s, m_s = refs[pos:pos + 3]
    c = pl.program_id(1)

    @pl.when(c == 0)
    def _():
        if has_init:
            c_s[...] = c0_ref[...]
            n_s[...] = n0_ref[...]
            m_s[...] = m0_ref[...]
        else:
            c_s[...] = jnp.zeros_like(c_s)
            n_s[...] = jnp.zeros_like(n_s)
            m_s[...] = jnp.zeros_like(m_s)

    t_idx = lax.broadcasted_iota(jnp.int32, (ML_CHUNK, ML_CHUNK), 0)
    s_idx = lax.broadcasted_iota(jnp.int32, (ML_CHUNK, ML_CHUNK), 1)
    eye = t_idx == s_idx
    lower = s_idx <= t_idx
    upper = s_idx >= t_idx
    for d, (q_ref, k_ref, v_ref, g_ref, h_ref) in enumerate(((qf, kf, vf, gf, hf_ref), (qb, kb, vb, gb, hb_ref))):
        vis, vis_t = (lower, upper) if d == 0 else (upper, lower)
        gates = g_ref[...]
        log_f = _log_sigmoid(gates)
        for h in range(ML_HEADS):
            sl = slice(h * ML_HEAD_DIM, (h + 1) * ML_HEAD_DIM)
            ci = 2 * d * ML_HEADS + h
            cf = (2 * d + 1) * ML_HEADS + h
            out, c_new, n_new, m_new = _mlstm_chunk(
                q_ref[:, sl], k_ref[:, sl], v_ref[:, sl], gates[:, ci:ci + 1], log_f[:, cf:cf + 1],
                c_s[d, h], n_s[d, h], m_s[d, h], eye, vis, vis_t)
            h_ref[:, sl] = out
            c_s[d, h] = c_new
            n_s[d, h] = n_new
            m_s[d, h] = m_new

    if emit_state:
        @pl.when(c == pl.num_programs(1) - 1)
        def _():
            co_ref[...] = c_s[...]
            no_ref[...] = n_s[...]
            mo_ref[...] = m_s[...]


def _mlstm(proj, gates, row0, n_seq, seq, layer, init=None, prev_h=None, emit_state=False, prev_state=None):
    nc = seq // ML_CHUNK
    dh = ML_HEAD_DIM
    rb = row0 // ML_CHUNK
    m_tot = proj.shape[0]

    def chunk_row(b, c, rev):
        return rb + b * nc + ((nc - 1 - c) if rev else c)

    def tok(col, rev):
        return pl.BlockSpec((ML_CHUNK, ML_WIDTH), lambda b, c: (chunk_row(b, c, rev), col // ML_WIDTH))

    def gat(rev):
        return pl.BlockSpec((ML_CHUNK, GATE_PAD), lambda b, c: (chunk_row(b, c, rev), 0))

    def hout(rev):
        return pl.BlockSpec((ML_CHUNK, ML_WIDTH), lambda b, c: (chunk_row(b, c, rev), 0))

    in_specs = [tok(COL_ML_Q, False), tok(COL_ML_K, False), tok(COL_ML_V, False), gat(False),
                tok(COL_ML_Q, True), tok(COL_ML_K, True), tok(COL_ML_V, True), gat(True)]
    args = [proj, proj, proj, gates, proj, proj, proj, gates]
    state_idx = lambda b, c: (b, layer, 0, 0, 0, 0)
    if init is not None:
        c0, n0, m0 = init
        in_specs += [_resident((None, None, 2, ML_HEADS, dh, dh), state_idx),
                     _resident((None, None, 2, ML_HEADS, 1, dh), state_idx),
                     _resident((None, None, 2, ML_HEADS, 1, 1), state_idx)]
        args += [c0, n0, m0]
    aliases = {}
    prevs = list(prev_h or ()) + list(prev_state or ())
    out_base = 0 if prev_h else 2
    for i, p in enumerate(prevs):
        aliases[len(args)] = out_base + i
        in_specs.append(_ANY)
        args.append(p)
    out_shape = [jax.ShapeDtypeStruct((m_tot, ML_WIDTH), F32)] * 2
    out_specs = [hout(False), hout(True)]
    if emit_state:
        out_shape += [jax.ShapeDtypeStruct((n_seq, DEPTH, 2, ML_HEADS, dh, dh), F32),
                      jax.ShapeDtypeStruct((n_seq, DEPTH, 2, ML_HEADS, 1, dh), F32),
                      jax.ShapeDtypeStruct((n_seq, DEPTH, 2, ML_HEADS, 1, 1), F32)]
        out_specs += [pl.BlockSpec((None, None, 2, ML_HEADS, dh, dh), state_idx),
                      pl.BlockSpec((None, None, 2, ML_HEADS, 1, dh), state_idx),
                      pl.BlockSpec((None, None, 2, ML_HEADS, 1, 1), state_idx)]
    return pl.pallas_call(
        functools.partial(_mlstm_kernel, has_init=init is not None, n_prev=len(prevs), emit_state=emit_state),
        out_shape=out_shape,
        grid=(n_seq, nc),
        in_specs=in_specs,
        out_specs=out_specs,
        input_output_aliases=aliases,
        scratch_shapes=[pltpu.VMEM((2, ML_HEADS, dh, dh), F32), pltpu.VMEM((2, ML_HEADS, 1, dh), F32),
                        pltpu.VMEM((2, ML_HEADS, 1, 1), F32)],
        compiler_params=_params(("parallel", "arbitrary")),
        name="mlstm",
    )(*args)


def _ml_post_kernel(hf_ref, hb_ref, o_ref, g_ref, out_ref):
    for h in range(ML_HEADS):
        sl = slice(h * ML_HEAD_DIM, (h + 1) * ML_HEAD_DIM)
        x = hf_ref[:, sl] + hb_ref[:, sl]
        x = x * lax.rsqrt(jnp.mean(x * x, axis=-1, keepdims=True) + EPS)
        x = x * g_ref[:, sl]
        out_ref[:, sl] = (jax.nn.sigmoid(o_ref[:, sl]) * x).astype(out_ref.dtype)


def _ml_post(hf, hb, proj, norm_g, bm=256):
    m = hf.shape[0]
    row = lambda i: (i, 0)
    return pl.pallas_call(
        _ml_post_kernel,
        out_shape=jax.ShapeDtypeStruct((m, ML_WIDTH), BF16),
        grid=(m // bm,),
        in_specs=[pl.BlockSpec((bm, ML_WIDTH), row), pl.BlockSpec((bm, ML_WIDTH), row),
                  pl.BlockSpec((bm, ML_WIDTH), lambda i: (i, COL_ML_O // ML_WIDTH)),
                  pl.BlockSpec((1, ML_WIDTH), lambda i: (0, 0))],
        out_specs=pl.BlockSpec((bm, ML_WIDTH), row),
        compiler_params=_params(("parallel",)),
        name="ml_post",
    )(hf, hb, proj, norm_g)


def kernel(x_prompt, x_sample, c, cache_na_k, cache_na_v, state_mlstm_C, state_mlstm_n, state_mlstm_m, c_ctx, w_ada, b_ada, norm_g, ffn_w_in, ffn_w_out, w_in, na_rel_bias, pool_w, pool_scale, ml_gate_bias, ml_norm_g, w_branch, w_out, final_norm_g):
    batch, seq, d = x_prompt.shape
    dec_batch, dec_seq, _ = x_sample.shape
    past = cache_na_k.shape[2]
    assert d == D_MODEL and dec_batch + 1 <= MOD_ROWS
    m_ctx = batch * seq
    m_lat = dec_batch * dec_seq
    groups = _Groups(m_ctx, dec_seq)

    x = jnp.concatenate([x_prompt.reshape(m_ctx, d), x_sample.reshape(m_lat, d)], axis=0)
    cond = jnp.concatenate([c_ctx[None], c, jnp.zeros((MOD_ROWS - 1 - dec_batch, d), F32)], axis=0)
    mod_all = _modulation(cond, w_ada, b_ada.reshape(DEPTH, 1, N_MOD * d))
    mod_all = mod_all.reshape(DEPTH, MOD_ROWS, N_MOD, d)

    ctx_k = cache_na_k.reshape(dec_batch, DEPTH, past, NA_WIDTH)
    ctx_v = cache_na_v.reshape(dec_batch, DEPTH, past, NA_WIDTH)
    init = (state_mlstm_C,
            state_mlstm_n.reshape(dec_batch, DEPTH, 2, ML_HEADS, 1, ML_HEAD_DIM),
            state_mlstm_m.reshape(dec_batch, DEPTH, 2, ML_HEADS, 1, 1))
    w_in_t = jnp.swapaxes(w_in, 1, 2)

    ks_out, vs_out = [], []
    states = None
    for l in range(DEPTH):
        mod = mod_all[l]
        w_merge = w_in_t[l, MAIN_COLS + GATE_COLS:, :].astype(BF16)
        b_gate = jnp.pad(ml_gate_bias[l], (0, GATE_PAD - GATE_COLS)).reshape(1, GATE_PAD)

        def ffn(x, j, shift_row):
            h = _norm_mod(x, norm_g[l, 2 * j].reshape(1, d), mod, groups, shift_row)
            act, w_o = _mm_swiglu(h, ffn_w_in, ffn_w_out, l, j)
            return _mm_resid(act, w_o, (), x, mod, groups, shift_row + 2, MACARON_W, bm=1024)

        x = ffn(x, 0, 0)

        h = _norm_mod(x, norm_g[l, 1].reshape(1, d), mod, groups, 3)
        proj = _mm_proj(h, w_in_t, l)
        gates = _mm_gates(h, w_in_t, l, b_gate)

        na_out = _ctx_attention(proj, batch, seq)
        na_out = _na_attention(proj, m_ctx, dec_batch, dec_seq, ctx_k, ctx_v, l, _rel_bias_table(na_rel_bias[l]),
                               na_out)
        ps = pool_scale[l].reshape(1, POOL_WIDTH)
        pool_out = _pool(proj, 0, batch, seq, pool_w[l], ps)
        pool_out = _pool(proj, m_ctx, dec_batch, dec_seq, pool_w[l], ps, prev=pool_out)
        hf, hb, *states = _mlstm(proj, gates, 0, batch, seq, l, emit_state=True, prev_state=states)
        hf, hb = _mlstm(proj, gates, m_ctx, dec_batch, dec_seq, l, init=init, prev_h=(hf, hb))
        ml_out = _ml_post(hf, hb, proj, ml_norm_g[l].reshape(1, ML_WIDTH))

        merged = _mm_merge(h, na_out, pool_out, ml_out, w_merge, w_branch, l)
        x = _mm_resid(merged, w_out, (l,), x, mod, groups, 5, 1.0, bm=2048)

        x = ffn(x, 1, 6)

        ks_out.append(proj[:m_ctx, COL_NA_K:COL_NA_K + NA_WIDTH].reshape(batch, seq, NA_HEADS, NA_HEAD_DIM))
        vs_out.append(proj[:m_ctx, COL_NA_V:COL_NA_V + NA_WIDTH].reshape(batch, seq, NA_HEADS, NA_HEAD_DIM))

    g_fin = final_norm_g.reshape(1, d)
    y_prompt = _final_norm(x, 0, m_ctx, g_fin).reshape(batch, seq, d)
    y_sample = _final_norm(x, m_ctx, m_lat, g_fin).reshape(dec_batch, dec_seq, d)
    c_fin, n_fin, m_fin = states
    return (y_prompt, y_sample, jnp.stack(ks_out, axis=1), jnp.stack(vs_out, axis=1),
            c_fin, n_fin.reshape(batch, DEPTH, 2, ML_HEADS, ML_HEAD_DIM), m_fin.reshape(batch, DEPTH, 2, ML_HEADS))
```

```python
import functools

import jax
import jax.numpy as jnp
import numpy as np
from jax import lax
from jax.experimental import pallas as pl
from jax.experimental.pallas import tpu as pltpu

F32 = jnp.float32
BF16 = jnp.bfloat16

D_MODEL = 4096
DEPTH = 2
GRID_W = 64
NA_HEADS = 8
NA_WIDTH = D_MODEL // 4
NA_HEAD_DIM = NA_WIDTH // NA_HEADS
NA_WIN_ROWS = 8
NA_WIN_COLS = 16
POOL_WINDOWS = (2, 4, 8, 16)
POOL_WIDTH = D_MODEL // 4
POOL_GROUP = POOL_WIDTH // 4
ML_HEADS = 8
ML_WIDTH = D_MODEL // 2
ML_HEAD_DIM = ML_WIDTH // ML_HEADS
ML_CHUNK = 256
D_FF = 256 * ((8 * D_MODEL // 3 + 255) // 256)
N_MOD = 9
MACARON_W = 0.5
EPS = 1e-6
MASK_VALUE = -1e30

COL_NA_Q = 0
COL_NA_K = NA_WIDTH
COL_NA_V = 2 * NA_WIDTH
COL_POOL = 3 * NA_WIDTH
COL_ML_Q = COL_POOL + POOL_WIDTH
COL_ML_K = COL_ML_Q + ML_WIDTH
COL_ML_V = COL_ML_K + ML_WIDTH
COL_ML_O = COL_ML_V + ML_WIDTH
MAIN_COLS = COL_ML_O + ML_WIDTH
GATE_COLS = 4 * ML_HEADS
GATE_PAD = 128

MOD_ROWS = 8
VMEM_LIMIT = 56 * 1024 * 1024


def _params(sem, vmem=VMEM_LIMIT):
    return pltpu.CompilerParams(dimension_semantics=sem, vmem_limit_bytes=vmem)


def _resident(block_shape, index_map):
    return pl.BlockSpec(block_shape, index_map, pipeline_mode=pl.Buffered(1))


_ANY = pl.BlockSpec(memory_space=pl.ANY)


def _modulation_kernel(c_ref, w_ref, b_ref, o_ref):
    c = c_ref[...]
    s = (c * jax.nn.sigmoid(c)).astype(BF16)
    o_ref[...] = jnp.dot(s, w_ref[...].astype(BF16), preferred_element_type=F32) + b_ref[...]


def _modulation(cond, w_ada, b_ada, bn=512):
    n = w_ada.shape[-1]
    return pl.pallas_call(
        _modulation_kernel,
        out_shape=jax.ShapeDtypeStruct((DEPTH, MOD_ROWS, n), F32),
        grid=(DEPTH, n // bn),
        in_specs=[
            pl.BlockSpec((MOD_ROWS, D_MODEL), lambda l, j: (0, 0)),
            pl.BlockSpec((None, D_MODEL, bn), lambda l, j: (l, 0, j)),
            pl.BlockSpec((None, 1, bn), lambda l, j: (l, 0, j)),
        ],
        out_specs=pl.BlockSpec((None, MOD_ROWS, bn), lambda l, j: (l, 0, j)),
        compiler_params=_params(("parallel", "parallel")),
        name="modulation",
    )(cond, w_ada, b_ada)


def _norm_mod_kernel(x_ref, g_ref, mod_ref, o_ref, *, shift_row):
    x = x_ref[...]
    y = x * lax.rsqrt(jnp.mean(x * x, axis=-1, keepdims=True) + EPS)
    y = y * g_ref[...]
    shift = mod_ref[shift_row:shift_row + 1, :]
    scale = mod_ref[shift_row + 1:shift_row + 2, :]
    o_ref[...] = (y * (1 + scale) + shift).astype(o_ref.dtype)


def _final_norm_kernel(x_ref, g_ref, o_ref):
    x = x_ref[...]
    y = x * lax.rsqrt(jnp.mean(x * x, axis=-1, keepdims=True) + EPS)
    o_ref[...] = y * g_ref[...]


class _Groups:
    def __init__(self, m_ctx, dec_seq):
        self.m_ctx = m_ctx
        self.dec_seq = dec_seq

    def of_block(self, i, bm):
        assert self.m_ctx % bm == 0 and self.dec_seq % bm == 0
        return jnp.maximum((i * bm) // self.dec_seq - (self.m_ctx // self.dec_seq - 1), 0)


def _norm_mod(x, g, mod, groups, shift_row, bm=512):
    m, d = x.shape
    return pl.pallas_call(
        functools.partial(_norm_mod_kernel, shift_row=shift_row),
        out_shape=jax.ShapeDtypeStruct((m, d), BF16),
        grid=(m // bm,),
        in_specs=[
            pl.BlockSpec((bm, d), lambda i: (i, 0)),
            pl.BlockSpec((1, d), lambda i: (0, 0)),
            pl.BlockSpec((None, N_MOD, d), lambda i: (groups.of_block(i, bm), 0, 0)),
        ],
        out_specs=pl.BlockSpec((bm, d), lambda i: (i, 0)),
        compiler_params=_params(("parallel",)),
        name="norm_mod",
    )(x, g, mod)


def _final_norm(x, row0, rows, g, bm=512):
    d = x.shape[1]
    return pl.pallas_call(
        _final_norm_kernel,
        out_shape=jax.ShapeDtypeStruct((rows, d), F32),
        grid=(rows // bm,),
        in_specs=[pl.BlockSpec((bm, d), lambda i: (row0 // bm + i, 0)), pl.BlockSpec((1, d), lambda i: (0, 0))],
        out_specs=pl.BlockSpec((bm, d), lambda i: (i, 0)),
        compiler_params=_params(("parallel",)),
        name="final_norm",
    )(x, g)


def _dot_nt(a, b):
    return lax.dot_general(a, b, (((1,), (1,)), ((), ())), preferred_element_type=F32)


def _mm_proj_kernel(h_ref, wt_ref, wg_lo_ref, wg_hi_ref, o_ref, wg_bf16_ref):
    half = wg_lo_ref.shape[0]
    wg_bf16_ref[:half, :] = wg_lo_ref[...].astype(BF16)
    wg_bf16_ref[half:, :] = wg_hi_ref[...].astype(BF16)
    o_ref[...] = _dot_nt(h_ref[...], wt_ref[...].astype(BF16))


def _mm_bias_kernel(h_ref, wt_ref, b_ref, o_ref):
    o_ref[...] = _dot_nt(h_ref[...], wt_ref[...].astype(BF16)) + b_ref[...]


def _mm_swiglu_kernel(h_ref, wa_ref, wb_ref, wo_ref, o_ref, wo_bf16_ref):
    wo_bf16_ref[...] = wo_ref[...].astype(BF16)
    h = h_ref[...]
    a = jnp.dot(h, wa_ref[...].astype(BF16), preferred_element_type=F32)
    b = jnp.dot(h, wb_ref[...].astype(BF16), preferred_element_type=F32)
    o_ref[...] = ((a * jax.nn.sigmoid(a)) * b).astype(o_ref.dtype)


def _mm_resid_kernel(a_ref, w_ref, x_ref, mod_ref, o_ref, *, gate_row, coef):
    y = jnp.dot(a_ref[...], w_ref[...].astype(BF16), preferred_element_type=F32)
    gate = mod_ref[gate_row:gate_row + 1, :]
    o_ref[...] = x_ref[...] + (coef * gate) * y


def _mm_merge_kernel(h_ref, na_ref, po_ref, ml_ref, wg_na, wg_po, wg_ml, wb_na, wb_po, wb_ml, o_ref):
    h = h_ref[...]

    def branch(x_ref, wg_ref, wb_ref):
        g = _dot_nt(h, wg_ref[...])
        y = jnp.dot(x_ref[...], wb_ref[...].astype(BF16), preferred_element_type=F32)
        return jax.nn.sigmoid(g) * y

    o = branch(na_ref, wg_na, wb_na) + branch(po_ref, wg_po, wb_po) + branch(ml_ref, wg_ml, wb_ml)
    o_ref[...] = o.astype(o_ref.dtype)


def _mm_proj(h, w_in_t, layer, bm=2048, bn=256):
    m, k = h.shape
    nb = MAIN_COLS // bn
    n_merge = w_in_t.shape[1] - MAIN_COLS - GATE_COLS
    slab = n_merge // ((m // bm) * nb)
    half = GATE_COLS
    assert slab * (m // bm) * nb == n_merge and slab == 2 * half and (MAIN_COLS + GATE_COLS) % half == 0
    first = (MAIN_COLS + GATE_COLS) // half
    return pl.pallas_call(
        _mm_proj_kernel,
        out_shape=[jax.ShapeDtypeStruct((m, MAIN_COLS), F32), jax.ShapeDtypeStruct((n_merge, k), BF16)],
        grid=(m // bm, nb),
        in_specs=[_resident((bm, k), lambda i, j: (i, 0)),
                  pl.BlockSpec((None, bn, k), lambda i, j: (layer, j, 0)),
                  pl.BlockSpec((None, half, k), lambda i, j: (layer, first + 2 * (i * nb + j), 0)),
                  pl.BlockSpec((None, half, k), lambda i, j: (layer, first + 2 * (i * nb + j) + 1, 0))],
        out_specs=[pl.BlockSpec((bm, bn), lambda i, j: (i, j)),
                   pl.BlockSpec((slab, k), lambda i, j: (i * nb + j, 0))],
        compiler_params=_params(("parallel", "arbitrary")),
        name="mm_proj",
    )(h, w_in_t, w_in_t, w_in_t)


def _mm_gates(h, w_in_t, layer, b, bm=1024):
    m, k = h.shape
    assert MAIN_COLS % GATE_PAD == 0
    return pl.pallas_call(
        _mm_bias_kernel,
        out_shape=jax.ShapeDtypeStruct((m, GATE_PAD), F32),
        grid=(m // bm,),
        in_specs=[pl.BlockSpec((bm, k), lambda i: (i, 0)),
                  pl.BlockSpec((None, GATE_PAD, k), lambda i: (layer, MAIN_COLS // GATE_PAD, 0)),
                  pl.BlockSpec((1, GATE_PAD), lambda i: (0, 0))],
        out_specs=pl.BlockSpec((bm, GATE_PAD), lambda i: (i, 0)),
        compiler_params=_params(("parallel",)),
        name="mm_gates",
    )(h, w_in_t, b)


def _mm_swiglu(h, ffn_w_in, ffn_w_out, layer, j, bm=2048, bn=256):
    m, k = h.shape
    nb = D_FF // bn
    steps = (m // bm) * nb
    slab = D_FF // steps
    assert slab * steps == D_FF and slab % 16 == 0
    d_out = ffn_w_out.shape[-1]
    return pl.pallas_call(
        _mm_swiglu_kernel,
        out_shape=[jax.ShapeDtypeStruct((m, D_FF), BF16), jax.ShapeDtypeStruct((D_FF, d_out), BF16)],
        grid=(m // bm, nb),
        in_specs=[_resident((bm, k), lambda i, n: (i, 0)),
                  pl.BlockSpec((None, None, k, bn), lambda i, n: (layer, j, 0, n)),
                  pl.BlockSpec((None, None, k, bn), lambda i, n: (layer, j, 0, n + nb)),
                  pl.BlockSpec((None, None, slab, d_out), lambda i, n: (layer, j, i * nb + n, 0))],
        out_specs=[pl.BlockSpec((bm, bn), lambda i, n: (i, n)),
                   pl.BlockSpec((slab, d_out), lambda i, n: (i * nb + n, 0))],
        compiler_params=_params(("parallel", "arbitrary")),
        name="mm_swiglu",
    )(h, ffn_w_in, ffn_w_in, ffn_w_out)


def _mm_resid(a, w, w_index, x, mod, groups, gate_row, coef, bm, bn=256):
    m, k = a.shape
    n = w.shape[-1]
    lead = (None,) * len(w_index)
    return pl.pallas_call(
        functools.partial(_mm_resid_kernel, gate_row=gate_row, coef=coef),
        out_shape=jax.ShapeDtypeStruct((m, n), F32),
        grid=(m // bm, n // bn),
        in_specs=[_resident((bm, k), lambda i, j: (i, 0)),
                  pl.BlockSpec(lead + (k, bn), lambda i, j: w_index + (0, j)),
                  pl.BlockSpec((bm, bn), lambda i, j: (i, j)),
                  pl.BlockSpec((None, N_MOD, bn), lambda i, j: (groups.of_block(i, bm), 0, j))],
        out_specs=pl.BlockSpec((bm, bn), lambda i, j: (i, j)),
        compiler_params=_params(("parallel", "arbitrary")),
        name="mm_resid",
    )(a, w, x, mod)


def _mm_merge(h, na, po, ml, wg, w_branch, layer, bm=1024, bn=256):
    m, d = h.shape
    nb = d // bn
    row = lambda i, j: (i, 0)
    return pl.pallas_call(
        _mm_merge_kernel,
        out_shape=jax.ShapeDtypeStruct((m, d), BF16),
        grid=(m // bm, nb),
        in_specs=[_resident((bm, d), row),
                  _resident((bm, NA_WIDTH), row),
                  _resident((bm, POOL_WIDTH), row),
                  _resident((bm, ML_WIDTH), row),
                  pl.BlockSpec((bn, d), lambda i, j: (j, 0)),
                  pl.BlockSpec((bn, d), lambda i, j: (j + nb, 0)),
                  pl.BlockSpec((bn, d), lambda i, j: (j + 2 * nb, 0)),
                  pl.BlockSpec((None, NA_WIDTH, bn), lambda i, j: (layer, 0, j)),
                  pl.BlockSpec((None, POOL_WIDTH, bn), lambda i, j: (layer, NA_WIDTH // POOL_WIDTH, j)),
                  pl.BlockSpec((None, ML_WIDTH, bn), lambda i, j: (layer, (NA_WIDTH + POOL_WIDTH) // ML_WIDTH, j))],
        out_specs=pl.BlockSpec((bm, bn), lambda i, j: (i, j)),
        compiler_params=_params(("parallel", "arbitrary")),
        name="mm_merge",
    )(h, na, po, ml, wg, wg, wg, w_branch, w_branch, w_branch)


def _softmax_rows(parts):
    m = parts[0].max(axis=-1, keepdims=True)
    for s in parts[1:]:
        m = jnp.maximum(m, s.max(axis=-1, keepdims=True))
    es = [jnp.exp(s - m) for s in parts]
    den = es[0].sum(axis=-1, keepdims=True)
    for e in es[1:]:
        den = den + e.sum(axis=-1, keepdims=True)
    return [e / den for e in es]


def _ctx_attn_kernel(q_ref, k_ref, v_ref, o_ref):
    scale = NA_HEAD_DIM ** -0.5
    for h in range(NA_HEADS):
        sl = slice(h * NA_HEAD_DIM, (h + 1) * NA_HEAD_DIM)
        q = q_ref[:, sl].astype(BF16)
        k = k_ref[:, sl].astype(BF16)
        v = v_ref[:, sl].astype(BF16)
        (p,) = _softmax_rows([_dot_nt(q, k) * scale])
        o_ref[:, sl] = jnp.dot(p.astype(BF16), v, preferred_element_type=F32).astype(o_ref.dtype)


def _ctx_attention(proj, n_seq, seq):
    blk = lambda c: pl.BlockSpec((seq, NA_WIDTH), lambda b: (b, c // NA_WIDTH))
    return pl.pallas_call(
        _ctx_attn_kernel,
        out_shape=jax.ShapeDtypeStruct((proj.shape[0], NA_WIDTH), BF16),
        grid=(n_seq,),
        in_specs=[blk(COL_NA_Q), blk(COL_NA_K), blk(COL_NA_V)],
        out_specs=pl.BlockSpec((seq, NA_WIDTH), lambda b: (b, 0)),
        compiler_params=_params(("parallel",)),
        name="ctx_attention",
    )(proj, proj, proj)


def _na_attn_kernel(q_ref, kf_ref, vf_ref, ckf_ref, cvf_ref, bias_ref, prev_ref, o_ref,
                    k_ref, v_ref, ck_ref, cv_ref, *, rows):
    del prev_ref
    r = pl.program_id(1)

    @pl.when(r == 0)
    def _():
        k_ref[...] = kf_ref[...].astype(BF16)
        v_ref[...] = vf_ref[...].astype(BF16)
        ck_ref[...] = ckf_ref[...].astype(BF16)
        cv_ref[...] = cvf_ref[...].astype(BF16)

    kr = min(NA_WIN_ROWS, rows)
    n_loc = kr * GRID_W
    start = jnp.clip(r - kr // 2, 0, rows - kr)
    k0 = pl.multiple_of(start * GRID_W, GRID_W)
    dr0 = start - r + (NA_WIN_ROWS - 1)
    scale = NA_HEAD_DIM ** -0.5
    wq = lax.broadcasted_iota(jnp.int32, (GRID_W, n_loc), 0)
    xk = lax.broadcasted_iota(jnp.int32, (GRID_W, n_loc), 1) % GRID_W
    cs = jnp.clip(wq - NA_WIN_COLS // 2, 0, GRID_W - NA_WIN_COLS)
    col_mask = (xk >= cs) & (xk < cs + NA_WIN_COLS)
    for h in range(NA_HEADS):
        sl = slice(h * NA_HEAD_DIM, (h + 1) * NA_HEAD_DIM)
        q = q_ref[:, sl].astype(BF16)
        kl = k_ref[pl.ds(k0, n_loc), sl]
        vl = v_ref[pl.ds(k0, n_loc), sl]
        bias = jnp.concatenate([bias_ref[h, dr0 + i] for i in range(kr)], axis=-1)
        s_loc = _dot_nt(q, kl) * scale + bias
        s_loc = jnp.where(col_mask, s_loc, MASK_VALUE)
        s_ctx = _dot_nt(q, ck_ref[:, sl]) * scale
        p_loc, p_ctx = _softmax_rows([s_loc, s_ctx])
        out = (jnp.dot(p_loc.astype(BF16), vl, preferred_element_type=F32)
               + jnp.dot(p_ctx.astype(BF16), cv_ref[:, sl], preferred_element_type=F32))
        o_ref[:, sl] = out.astype(o_ref.dtype)


def _na_attention(proj, row0, n_seq, seq, ctx_k, ctx_v, layer, bias_tab, prev):
    rows = seq // GRID_W
    past = ctx_k.shape[2]
    rb = row0 // GRID_W
    q_spec = pl.BlockSpec((GRID_W, NA_WIDTH), lambda b, r: (rb + b * rows + r, COL_NA_Q // NA_WIDTH))
    kv = lambda c: _resident((seq, NA_WIDTH), lambda b, r: (row0 // seq + b, c // NA_WIDTH))
    ctx = _resident((None, None, past, NA_WIDTH), lambda b, r: (b, layer, 0, 0))
    return pl.pallas_call(
        functools.partial(_na_attn_kernel, rows=rows),
        out_shape=jax.ShapeDtypeStruct(prev.shape, prev.dtype),
        grid=(n_seq, rows),
        in_specs=[q_spec, kv(COL_NA_K), kv(COL_NA_V), ctx, ctx,
                  _resident(bias_tab.shape, lambda b, r: (0, 0, 0, 0)), _ANY],
        out_specs=pl.BlockSpec((GRID_W, NA_WIDTH), lambda b, r: (rb + b * rows + r, 0)),
        input_output_aliases={6: 0},
        scratch_shapes=[pltpu.VMEM((seq, NA_WIDTH), BF16), pltpu.VMEM((seq, NA_WIDTH), BF16),
                        pltpu.VMEM((past, NA_WIDTH), BF16), pltpu.VMEM((past, NA_WIDTH), BF16)],
        compiler_params=_params(("arbitrary", "arbitrary")),
        name="na_attention",
    )(proj, proj, proj, ctx_k, ctx_v, bias_tab, prev)


def _rel_bias_table(rel_bias):
    cq = np.arange(GRID_W)
    dc = np.clip(cq[None, :] - cq[:, None], -(NA_WIN_COLS - 1), NA_WIN_COLS - 1) + (NA_WIN_COLS - 1)
    return rel_bias[:, :, dc]


def _pool_kernel(u_ref, w_ref, s_ref, *rest, seq):
    o_ref = rest[-1]
    pos = lax.broadcasted_iota(jnp.int32, (seq, POOL_GROUP), 0)
    for g, win in enumerate(POOL_WINDOWS):
        sl = slice(g * POOL_GROUP, (g + 1) * POOL_GROUP)
        u = u_ref[:, sl]
        acc = jnp.zeros_like(u)
        for d in range(-(win // 2), win - win // 2):
            shifted = u if d == 0 else pltpu.roll(u, (-d) % seq, 0)
            valid = (pos + d >= 0) & (pos + d < seq)
            acc = acc + jnp.where(valid, shifted, 0.0)
        lo = jnp.clip(pos - win // 2, 0, seq)
        hi = jnp.clip(pos + win - win // 2, 0, seq)
        pooled = acc / (hi - lo).astype(F32) - u
        y = jnp.dot(pooled.astype(BF16), w_ref[g].astype(BF16), preferred_element_type=F32)
        o_ref[:, sl] = (y * s_ref[:, sl]).astype(o_ref.dtype)


def _pool(proj, row0, n_seq, seq, pool_w, pool_scale, prev=None):
    in_specs = [pl.BlockSpec((seq, POOL_WIDTH), lambda b: (row0 // seq + b, COL_POOL // POOL_WIDTH)),
                pl.BlockSpec(pool_w.shape, lambda b: (0, 0, 0)),
                pl.BlockSpec((1, POOL_WIDTH), lambda b: (0, 0))]
    args = [proj, pool_w, pool_scale]
    aliases = {}
    if prev is not None:
        in_specs.append(_ANY)
        args.append(prev)
        aliases = {3: 0}
    return pl.pallas_call(
        functools.partial(_pool_kernel, seq=seq),
        out_shape=jax.ShapeDtypeStruct((proj.shape[0], POOL_WIDTH), BF16),
        grid=(n_seq,),
        in_specs=in_specs,
        out_specs=pl.BlockSpec((seq, POOL_WIDTH), lambda b: (row0 // seq + b, 0)),
        input_output_aliases=aliases,
        compiler_params=_params(("parallel",)),
        name="pool",
    )(*args)


def _log_sigmoid(x):
    return jnp.minimum(x, 0.0) - jnp.log1p(jnp.exp(-jnp.abs(x)))


def _scan_cumsum(x, reverse, row_idx):
    n = x.shape[0]
    s = 1
    while s < n:
        if reverse:
            x = x + jnp.where(row_idx < n - s, pltpu.roll(x, n - s, 0), 0.0)
        else:
            x = x + jnp.where(row_idx >= s, pltpu.roll(x, s, 0), 0.0)
        s *= 2
    return x


def _mlstm_chunk(q, k, v, i_col, b_col, r_row, total, c_prev, n_prev, m_prev, vis):
    w = jnp.where(vis, r_row, -jnp.inf)
    inter = b_col + m_prev
    m_t = jnp.maximum(inter, b_col + jnp.max(w, axis=1, keepdims=True))
    w_inter = jnp.exp(inter - m_t)
    ks = k * (ML_HEAD_DIM ** -0.5)
    qb = q.astype(BF16)
    vb = v.astype(BF16)
    qk = _dot_nt(qb, ks.astype(BF16)) * jnp.exp(w + (b_col - m_t))
    num = (jnp.dot(qk.astype(BF16), vb, preferred_element_type=F32)
           + w_inter * jnp.dot(qb, c_prev.astype(BF16), preferred_element_type=F32))
    den = jnp.sum(qk, axis=1, keepdims=True) + w_inter * jnp.sum(q * n_prev, axis=1, keepdims=True)
    h = num * (1.0 / jnp.maximum(jnp.abs(den), jnp.exp(-m_t)))
    g = total - b_col + i_col
    m_new = jnp.maximum(total + m_prev, jnp.max(g, axis=0, keepdims=True))
    ws = jnp.exp(g - m_new)
    decay = jnp.exp(total + m_prev - m_new)
    kw = ks * ws
    c_new = decay * c_prev + lax.dot_general(kw.astype(BF16), vb, (((0,), (0,)), ((), ())),
                                             preferred_element_type=F32)
    n_new = decay * n_prev + jnp.sum(kw, axis=0, keepdims=True)
    return h, c_new, n_new, m_new


def _mlstm_kernel(*refs, has_init, n_prev, emit_state):
    qf, kf, vf, gf, qb, kb, vb, gb = refs[:8]
    pos = 8
    if has_init:
        c0_ref, n0_ref, m0_ref = refs[pos:pos + 3]
        pos += 3
    pos += n_prev
    hf_ref, hb_ref = refs[pos:pos + 2]
    pos += 2
    if emit_state:
        co_ref, no_ref, mo_ref = refs[pos:pos + 3]
        pos += 3
    c_s, n_s, m_s = refs[pos:pos + 3]
    c = pl.program_id(1)

    @pl.when(c == 0)
    def _():
        if has_init:
            c_s[...] = c0_ref[...]
            n_s[...] = n0_ref[...]
            m_s[...] = m0_ref[...]
        else:
            c_s[...] = jnp.zeros_like(c_s)
            n_s[...] = jnp.zeros_like(n_s)
            m_s[...] = jnp.zeros_like(m_s)

    t_idx = lax.broadcasted_iota(jnp.int32, (ML_CHUNK, ML_CHUNK), 0)
    s_idx = lax.broadcasted_iota(jnp.int32, (ML_CHUNK, ML_CHUNK), 1)
    row_idx = lax.broadcasted_iota(jnp.int32, (ML_CHUNK, GATE_PAD), 0)
    for d, (q_ref, k_ref, v_ref, g_ref, h_ref) in enumerate(((qf, kf, vf, gf, hf_ref), (qb, kb, vb, gb, hb_ref))):
        reverse = d == 1
        vis = (s_idx >= t_idx) if reverse else (s_idx <= t_idx)
        gates = g_ref[...]
        cum_f = _scan_cumsum(_log_sigmoid(gates), reverse, row_idx)
        total_row = cum_f[0:1, :] if reverse else cum_f[ML_CHUNK - 1:ML_CHUNK, :]
        gates_t = gates.T
        cum_f_t = cum_f.T
        for h in range(ML_HEADS):
            sl = slice(h * ML_HEAD_DIM, (h + 1) * ML_HEAD_DIM)
            ci = 2 * d * ML_HEADS + h
            cf = (2 * d + 1) * ML_HEADS + h
            out, c_new, n_new, m_new = _mlstm_chunk(
                q_ref[:, sl], k_ref[:, sl], v_ref[:, sl], gates[:, ci:ci + 1], cum_f[:, cf:cf + 1],
                gates_t[ci:ci + 1, :] - cum_f_t[cf:cf + 1, :], total_row[:, cf:cf + 1],
                c_s[d, h], n_s[d, h], m_s[d, h], vis)
            h_ref[:, sl] = out
            c_s[d, h] = c_new
            n_s[d, h] = n_new
            m_s[d, h] = m_new

    if emit_state:
        @pl.when(c == pl.num_programs(1) - 1)
        def _():
            co_ref[...] = c_s[...]
            no_ref[...] = n_s[...]
            mo_ref[...] = m_s[...]


def _mlstm(proj, gates, row0, n_seq, seq, layer, init=None, prev_h=None, emit_state=False, prev_state=None):
    nc = seq // ML_CHUNK
    dh = ML_HEAD_DIM
    rb = row0 // ML_CHUNK
    m_tot = proj.shape[0]

    def chunk_row(b, c, rev):
        return rb + b * nc + ((nc - 1 - c) if rev else c)

    def tok(col, rev):
        return pl.BlockSpec((ML_CHUNK, ML_WIDTH), lambda b, c: (chunk_row(b, c, rev), col // ML_WIDTH))

    def gat(rev):
        return pl.BlockSpec((ML_CHUNK, GATE_PAD), lambda b, c: (chunk_row(b, c, rev), 0))

    def hout(rev):
        return pl.BlockSpec((ML_CHUNK, ML_WIDTH), lambda b, c: (chunk_row(b, c, rev), 0))

    in_specs = [tok(COL_ML_Q, False), tok(COL_ML_K, False), tok(COL_ML_V, False), gat(False),
                tok(COL_ML_Q, True), tok(COL_ML_K, True), tok(COL_ML_V, True), gat(True)]
    args = [proj, proj, proj, gates, proj, proj, proj, gates]
    state_idx = lambda b, c: (b, layer, 0, 0, 0, 0)
    if init is not None:
        c0, n0, m0 = init
        in_specs += [_resident((None, None, 2, ML_HEADS, dh, dh), state_idx),
                     _resident((None, None, 2, ML_HEADS, 1, dh), state_idx),
                     _resident((None, None, 2, ML_HEADS, 1, 1), state_idx)]
        args += [c0, n0, m0]
    aliases = {}
    prevs = list(prev_h or ()) + list(prev_state or ())
    out_base = 0 if prev_h else 2
    for i, p in enumerate(prevs):
        aliases[len(args)] = out_base + i
        in_specs.append(_ANY)
        args.append(p)
    out_shape = [jax.ShapeDtypeStruct((m_tot, ML_WIDTH), F32)] * 2
    out_specs = [hout(False), hout(True)]
    if emit_state:
        out_shape += [jax.ShapeDtypeStruct((n_seq, DEPTH, 2, ML_HEADS, dh, dh), F32),
                      jax.ShapeDtypeStruct((n_seq, DEPTH, 2, ML_HEADS, 1, dh), F32),
                      jax.ShapeDtypeStruct((n_seq, DEPTH, 2, ML_HEADS, 1, 1), F32)]
        out_specs += [pl.BlockSpec((None, None, 2, ML_HEADS, dh, dh), state_idx),
                      pl.BlockSpec((None, None, 2, ML_HEADS, 1, dh), state_idx),
                      pl.BlockSpec((None, None, 2, ML_HEADS, 1, 1), state_idx)]
    return pl.pallas_call(
        functools.partial(_mlstm_kernel, has_init=init is not None, n_prev=len(prevs), emit_state=emit_state),
        out_shape=out_shape,
        grid=(n_seq, nc),
        in_specs=in_specs,
        out_specs=out_specs,
        input_output_aliases=aliases,
        scratch_shapes=[pltpu.VMEM((2, ML_HEADS, dh, dh), F32), pltpu.VMEM((2, ML_HEADS, 1, dh), F32),
                        pltpu.VMEM((2, ML_HEADS, 1, 1), F32)],
        compiler_params=_params(("parallel", "arbitrary")),
        name="mlstm",
    )(*args)


def _ml_post_kernel(hf_ref, hb_ref, o_ref, g_ref, out_ref):
    for h in range(ML_HEADS):
        sl = slice(h * ML_HEAD_DIM, (h + 1) * ML_HEAD_DIM)
        x = hf_ref[:, sl] + hb_ref[:, sl]
        x = x * lax.rsqrt(jnp.mean(x * x, axis=-1, keepdims=True) + EPS)
        x = x * g_ref[:, sl]
        out_ref[:, sl] = (jax.nn.sigmoid(o_ref[:, sl]) * x).astype(out_ref.dtype)


def _ml_post(hf, hb, proj, norm_g, bm=512):
    m = hf.shape[0]
    row = lambda i: (i, 0)
    return pl.pallas_call(
        _ml_post_kernel,
        out_shape=jax.ShapeDtypeStruct((m, ML_WIDTH), BF16),
        grid=(m // bm,),
        in_specs=[pl.BlockSpec((bm, ML_WIDTH), row), pl.BlockSpec((bm, ML_WIDTH), row),
                  pl.BlockSpec((bm, ML_WIDTH), lambda i: (i, COL_ML_O // ML_WIDTH)),
                  pl.BlockSpec((1, ML_WIDTH), lambda i: (0, 0))],
        out_specs=pl.BlockSpec((bm, ML_WIDTH), row),
        compiler_params=_params(("parallel",)),
        name="ml_post",
    )(hf, hb, proj, norm_g)


def kernel(x_prompt, x_sample, c, cache_na_k, cache_na_v, state_mlstm_C, state_mlstm_n, state_mlstm_m, c_ctx, w_ada, b_ada, norm_g, ffn_w_in, ffn_w_out, w_in, na_rel_bias, pool_w, pool_scale, ml_gate_bias, ml_norm_g, w_branch, w_out, final_norm_g):
    batch, seq, d = x_prompt.shape
    dec_batch, dec_seq, _ = x_sample.shape
    past = cache_na_k.shape[2]
    assert d == D_MODEL and dec_batch + 1 <= MOD_ROWS
    m_ctx = batch * seq
    m_lat = dec_batch * dec_seq
    groups = _Groups(m_ctx, dec_seq)

    x = jnp.concatenate([x_prompt.reshape(m_ctx, d), x_sample.reshape(m_lat, d)], axis=0)
    cond = jnp.concatenate([c_ctx[None], c, jnp.zeros((MOD_ROWS - 1 - dec_batch, d), F32)], axis=0)
    mod_all = _modulation(cond, w_ada, b_ada.reshape(DEPTH, 1, N_MOD * d))
    mod_all = mod_all.reshape(DEPTH, MOD_ROWS, N_MOD, d)

    ctx_k = cache_na_k.reshape(dec_batch, DEPTH, past, NA_WIDTH)
    ctx_v = cache_na_v.reshape(dec_batch, DEPTH, past, NA_WIDTH)
    init = (state_mlstm_C,
            state_mlstm_n.reshape(dec_batch, DEPTH, 2, ML_HEADS, 1, ML_HEAD_DIM),
            state_mlstm_m.reshape(dec_batch, DEPTH, 2, ML_HEADS, 1, 1))
    w_in_t = jnp.swapaxes(w_in, 1, 2)

    ks_out, vs_out = [], []
    states = None
    for l in range(DEPTH):
        mod = mod_all[l]
        b_gate = jnp.pad(ml_gate_bias[l], (0, GATE_PAD - GATE_COLS)).reshape(1, GATE_PAD)

        def ffn(x, j, shift_row):
            h = _norm_mod(x, norm_g[l, 2 * j].reshape(1, d), mod, groups, shift_row)
            act, w_o = _mm_swiglu(h, ffn_w_in, ffn_w_out, l, j)
            return _mm_resid(act, w_o, (), x, mod, groups, shift_row + 2, MACARON_W, bm=1024)

        x = ffn(x, 0, 0)

        h = _norm_mod(x, norm_g[l, 1].reshape(1, d), mod, groups, 3)
        proj, w_merge = _mm_proj(h, w_in_t, l)
        gates = _mm_gates(h, w_in_t, l, b_gate)

        na_out = _ctx_attention(proj, batch, seq)
        na_out = _na_attention(proj, m_ctx, dec_batch, dec_seq, ctx_k, ctx_v, l, _rel_bias_table(na_rel_bias[l]),
                               na_out)
        ps = pool_scale[l].reshape(1, POOL_WIDTH)
        pool_out = _pool(proj, 0, batch, seq, pool_w[l], ps)
        pool_out = _pool(proj, m_ctx, dec_batch, dec_seq, pool_w[l], ps, prev=pool_out)
        hf, hb, *states = _mlstm(proj, gates, 0, batch, seq, l, emit_state=True, prev_state=states)
        hf, hb = _mlstm(proj, gates, m_ctx, dec_batch, dec_seq, l, init=init, prev_h=(hf, hb))
        ml_out = _ml_post(hf, hb, proj, ml_norm_g[l].reshape(1, ML_WIDTH))

        merged = _mm_merge(h, na_out, pool_out, ml_out, w_merge, w_branch, l)
        x = _mm_resid(merged, w_out, (l,), x, mod, groups, 5, 1.0, bm=2048)

        x = ffn(x, 1, 6)

        ks_out.append(proj[:m_ctx, COL_NA_K:COL_NA_K + NA_WIDTH].reshape(batch, seq, NA_HEADS, NA_HEAD_DIM))
        vs_out.append(proj[:m_ctx, COL_NA_V:COL_NA_V + NA_WIDTH].reshape(batch, seq, NA_HEADS, NA_HEAD_DIM))

    g_fin = final_norm_g.reshape(1, d)
    y_prompt = _final_norm(x, 0, m_ctx, g_fin).reshape(batch, seq, d)
    y_sample = _final_norm(x, m_ctx, m_lat, g_fin).reshape(dec_batch, dec_seq, d)
    c_fin, n_fin, m_fin = states
    return (y_prompt, y_sample, jnp.stack(ks_out, axis=1), jnp.stack(vs_out, axis=1),
            c_fin, n_fin.reshape(batch, DEPTH, 2, ML_HEADS, ML_HEAD_DIM), m_fin.reshape(batch, DEPTH, 2, ML_HEADS))
```

```python
import functools

import jax
import jax.numpy as jnp
import numpy as np
from jax import lax
from jax.experimental import pallas as pl
from jax.experimental.pallas import tpu as pltpu

F32 = jnp.float32
BF16 = jnp.bfloat16

D_MODEL = 4096
DEPTH = 2
GRID_W = 64
NA_HEADS = 8
NA_WIDTH = D_MODEL // 4
NA_HEAD_DIM = NA_WIDTH // NA_HEADS
NA_WIN_ROWS = 8
NA_WIN_COLS = 16
NA_ROWS_PER_STEP = 4
NA_SPAN_ROWS = NA_WIN_ROWS + NA_ROWS_PER_STEP
NA_MASKED_SLOT = 2 * NA_WIN_ROWS - 1
POOL_WINDOWS = (2, 4, 8, 16)
POOL_WIDTH = D_MODEL // 4
POOL_GROUP = POOL_WIDTH // 4
ML_HEADS = 8
ML_WIDTH = D_MODEL // 2
ML_HEAD_DIM = ML_WIDTH // ML_HEADS
ML_CHUNK = 256
D_FF = 256 * ((8 * D_MODEL // 3 + 255) // 256)
N_MOD = 9
MACARON_W = 0.5
EPS = 1e-6
MASK_VALUE = -1e30

COL_NA_Q = 0
COL_NA_K = NA_WIDTH
COL_NA_V = 2 * NA_WIDTH
COL_POOL = 3 * NA_WIDTH
COL_ML_Q = COL_POOL + POOL_WIDTH
COL_ML_K = COL_ML_Q + ML_WIDTH
COL_ML_V = COL_ML_K + ML_WIDTH
COL_ML_O = COL_ML_V + ML_WIDTH
MAIN_COLS = COL_ML_O + ML_WIDTH
GATE_COLS = 4 * ML_HEADS
GATE_PAD = 128

MOD_ROWS = 8
VMEM_LIMIT = 56 * 1024 * 1024


def _params(sem, vmem=VMEM_LIMIT):
    return pltpu.CompilerParams(dimension_semantics=sem, vmem_limit_bytes=vmem)


def _resident(block_shape, index_map):
    return pl.BlockSpec(block_shape, index_map, pipeline_mode=pl.Buffered(1))


DOUBLE_BUFFER_MAX_BYTES = 8 * 1024 * 1024


def _outer_block(block_shape, index_map, itemsize):
    nbytes = itemsize * int(np.prod([d for d in block_shape if d is not None]))
    if nbytes > DOUBLE_BUFFER_MAX_BYTES:
        return _resident(block_shape, index_map)
    return pl.BlockSpec(block_shape, index_map)


_ANY = pl.BlockSpec(memory_space=pl.ANY)


def _modulation_kernel(c_ref, w_ref, b_ref, o_ref):
    c = c_ref[...]
    s = (c * jax.nn.sigmoid(c)).astype(BF16)
    o_ref[...] = jnp.dot(s, w_ref[...].astype(BF16), preferred_element_type=F32) + b_ref[...]


def _modulation(cond, w_ada, b_ada, bn=512):
    n = w_ada.shape[-1]
    return pl.pallas_call(
        _modulation_kernel,
        out_shape=jax.ShapeDtypeStruct((DEPTH, MOD_ROWS, n), F32),
        grid=(DEPTH, n // bn),
        in_specs=[
            pl.BlockSpec((MOD_ROWS, D_MODEL), lambda l, j: (0, 0)),
            pl.BlockSpec((None, D_MODEL, bn), lambda l, j: (l, 0, j)),
            pl.BlockSpec((None, 1, bn), lambda l, j: (l, 0, j)),
        ],
        out_specs=pl.BlockSpec((None, MOD_ROWS, bn), lambda l, j: (l, 0, j)),
        compiler_params=_params(("parallel", "parallel")),
        name="modulation",
    )(cond, w_ada, b_ada)


def _norm_mod_kernel(x_ref, g_ref, mod_ref, o_ref, *, shift_row):
    x = x_ref[...]
    y = x * lax.rsqrt(jnp.mean(x * x, axis=-1, keepdims=True) + EPS)
    y = y * g_ref[...]
    shift = mod_ref[shift_row:shift_row + 1, :]
    scale = mod_ref[shift_row + 1:shift_row + 2, :]
    o_ref[...] = (y * (1 + scale) + shift).astype(o_ref.dtype)


def _final_norm_kernel(x_ref, g_ref, o_ref):
    x = x_ref[...]
    y = x * lax.rsqrt(jnp.mean(x * x, axis=-1, keepdims=True) + EPS)
    o_ref[...] = y * g_ref[...]


class _Groups:
    def __init__(self, m_ctx, dec_seq):
        self.m_ctx = m_ctx
        self.dec_seq = dec_seq

    def of_block(self, i, bm):
        assert self.m_ctx % bm == 0 and self.dec_seq % bm == 0
        return jnp.maximum((i * bm) // self.dec_seq - (self.m_ctx // self.dec_seq - 1), 0)


def _norm_mod(x, g, mod, groups, shift_row, bm=512):
    m, d = x.shape
    return pl.pallas_call(
        functools.partial(_norm_mod_kernel, shift_row=shift_row),
        out_shape=jax.ShapeDtypeStruct((m, d), BF16),
        grid=(m // bm,),
        in_specs=[
            pl.BlockSpec((bm, d), lambda i: (i, 0)),
            pl.BlockSpec((1, d), lambda i: (0, 0)),
            pl.BlockSpec((None, N_MOD, d), lambda i: (groups.of_block(i, bm), 0, 0)),
        ],
        out_specs=pl.BlockSpec((bm, d), lambda i: (i, 0)),
        compiler_params=_params(("parallel",)),
        name="norm_mod",
    )(x, g, mod)


def _final_norm(x, row0, rows, g, bm=512):
    d = x.shape[1]
    return pl.pallas_call(
        _final_norm_kernel,
        out_shape=jax.ShapeDtypeStruct((rows, d), F32),
        grid=(rows // bm,),
        in_specs=[pl.BlockSpec((bm, d), lambda i: (row0 // bm + i, 0)), pl.BlockSpec((1, d), lambda i: (0, 0))],
        out_specs=pl.BlockSpec((bm, d), lambda i: (i, 0)),
        compiler_params=_params(("parallel",)),
        name="final_norm",
    )(x, g)


def _dot_nt(a, b):
    return lax.dot_general(a, b, (((1,), (1,)), ((), ())), preferred_element_type=F32)


def _mm_proj_kernel(h_ref, wt_ref, wg_lo_ref, wg_hi_ref, o_ref, wg_bf16_ref):
    half = wg_lo_ref.shape[0]
    wg_bf16_ref[:half, :] = wg_lo_ref[...].astype(BF16)
    wg_bf16_ref[half:, :] = wg_hi_ref[...].astype(BF16)
    o_ref[...] = _dot_nt(h_ref[...], wt_ref[...].astype(BF16))


def _mm_bias_kernel(h_ref, wt_ref, b_ref, o_ref):
    o_ref[...] = _dot_nt(h_ref[...], wt_ref[...].astype(BF16)) + b_ref[...]


def _mm_swiglu_kernel(h_ref, wa_ref, wb_ref, wo_ref, o_ref, wo_bf16_ref):
    wo_bf16_ref[...] = wo_ref[...].astype(BF16)
    h = h_ref[...]
    a = jnp.dot(h, wa_ref[...].astype(BF16), preferred_element_type=F32)
    b = jnp.dot(h, wb_ref[...].astype(BF16), preferred_element_type=F32)
    o_ref[...] = ((a * jax.nn.sigmoid(a)) * b).astype(o_ref.dtype)


def _mm_resid_kernel(a_ref, w_ref, x_ref, mod_ref, o_ref, *, gate_row, coef):
    y = jnp.dot(a_ref[...], w_ref[...].astype(BF16), preferred_element_type=F32)
    gate = mod_ref[gate_row:gate_row + 1, :]
    o_ref[...] = x_ref[...] + (coef * gate) * y


def _mm_merge_kernel(h_ref, na_ref, po_ref, ml_ref, wg_na, wg_po, wg_ml, wb_na, wb_po, wb_ml, o_ref):
    h = h_ref[...]

    def branch(x_ref, wg_ref, wb_ref):
        g = _dot_nt(h, wg_ref[...])
        y = jnp.dot(x_ref[...], wb_ref[...].astype(BF16), preferred_element_type=F32)
        return jax.nn.sigmoid(g) * y

    o = branch(na_ref, wg_na, wb_na) + branch(po_ref, wg_po, wb_po) + branch(ml_ref, wg_ml, wb_ml)
    o_ref[...] = o.astype(o_ref.dtype)


def _mm_proj(h, w_in_t, layer, bm=1024, bn=512):
    m, k = h.shape
    nb = MAIN_COLS // bn
    n_merge = w_in_t.shape[1] - MAIN_COLS - GATE_COLS
    slab = n_merge // ((m // bm) * nb)
    half = GATE_COLS
    assert slab * (m // bm) * nb == n_merge and slab == 2 * half and (MAIN_COLS + GATE_COLS) % half == 0
    first = (MAIN_COLS + GATE_COLS) // half
    return pl.pallas_call(
        _mm_proj_kernel,
        out_shape=[jax.ShapeDtypeStruct((m, MAIN_COLS), F32), jax.ShapeDtypeStruct((n_merge, k), BF16)],
        grid=(m // bm, nb),
        in_specs=[_outer_block((bm, k), lambda i, j: (i, 0), h.dtype.itemsize),
                  pl.BlockSpec((None, bn, k), lambda i, j: (layer, j, 0)),
                  pl.BlockSpec((None, half, k), lambda i, j: (layer, first + 2 * (i * nb + j), 0)),
                  pl.BlockSpec((None, half, k), lambda i, j: (layer, first + 2 * (i * nb + j) + 1, 0))],
        out_specs=[pl.BlockSpec((bm, bn), lambda i, j: (i, j)),
                   pl.BlockSpec((slab, k), lambda i, j: (i * nb + j, 0))],
        compiler_params=_params(("parallel", "arbitrary")),
        name="mm_proj",
    )(h, w_in_t, w_in_t, w_in_t)


def _mm_gates(h, w_in_t, layer, b, bm=1024):
    m, k = h.shape
    assert MAIN_COLS % GATE_PAD == 0
    return pl.pallas_call(
        _mm_bias_kernel,
        out_shape=jax.ShapeDtypeStruct((m, GATE_PAD), F32),
        grid=(m // bm,),
        in_specs=[pl.BlockSpec((bm, k), lambda i: (i, 0)),
                  pl.BlockSpec((None, GATE_PAD, k), lambda i: (layer, MAIN_COLS // GATE_PAD, 0)),
                  pl.BlockSpec((1, GATE_PAD), lambda i: (0, 0))],
        out_specs=pl.BlockSpec((bm, GATE_PAD), lambda i: (i, 0)),
        compiler_params=_params(("parallel",)),
        name="mm_gates",
    )(h, w_in_t, b)


def _mm_swiglu(h, ffn_w_in, ffn_w_out, layer, j, bm=2048, bn=256):
    m, k = h.shape
    nb = D_FF // bn
    steps = (m // bm) * nb
    slab = D_FF // steps
    assert slab * steps == D_FF and slab % 16 == 0
    d_out = ffn_w_out.shape[-1]
    return pl.pallas_call(
        _mm_swiglu_kernel,
        out_shape=[jax.ShapeDtypeStruct((m, D_FF), BF16), jax.ShapeDtypeStruct((D_FF, d_out), BF16)],
        grid=(m // bm, nb),
        in_specs=[_resident((bm, k), lambda i, n: (i, 0)),
                  pl.BlockSpec((None, None, k, bn), lambda i, n: (layer, j, 0, n)),
                  pl.BlockSpec((None, None, k, bn), lambda i, n: (layer, j, 0, n + nb)),
                  pl.BlockSpec((None, None, slab, d_out), lambda i, n: (layer, j, i * nb + n, 0))],
        out_specs=[pl.BlockSpec((bm, bn), lambda i, n: (i, n)),
                   pl.BlockSpec((slab, d_out), lambda i, n: (i * nb + n, 0))],
        compiler_params=_params(("parallel", "arbitrary")),
        name="mm_swiglu",
    )(h, ffn_w_in, ffn_w_in, ffn_w_out)


def _mm_resid(a, w, w_index, x, mod, groups, gate_row, coef, bm, bn):
    m, k = a.shape
    n = w.shape[-1]
    lead = (None,) * len(w_index)
    return pl.pallas_call(
        functools.partial(_mm_resid_kernel, gate_row=gate_row, coef=coef),
        out_shape=jax.ShapeDtypeStruct((m, n), F32),
        grid=(m // bm, n // bn),
        in_specs=[_outer_block((bm, k), lambda i, j: (i, 0), a.dtype.itemsize),
                  pl.BlockSpec(lead + (k, bn), lambda i, j: w_index + (0, j)),
                  pl.BlockSpec((bm, bn), lambda i, j: (i, j)),
                  pl.BlockSpec((None, N_MOD, bn), lambda i, j: (groups.of_block(i, bm), 0, j))],
        out_specs=pl.BlockSpec((bm, bn), lambda i, j: (i, j)),
        compiler_params=_params(("parallel", "arbitrary")),
        name="mm_resid",
    )(a, w, x, mod)


def _mm_merge(h, na, po, ml, wg, w_branch, layer, bm=1024, bn=256):
    m, d = h.shape
    nb = d // bn
    row = lambda i, j: (i, 0)
    return pl.pallas_call(
        _mm_merge_kernel,
        out_shape=jax.ShapeDtypeStruct((m, d), BF16),
        grid=(m // bm, nb),
        in_specs=[_resident((bm, d), row),
                  _resident((bm, NA_WIDTH), row),
                  _resident((bm, POOL_WIDTH), row),
                  _resident((bm, ML_WIDTH), row),
                  pl.BlockSpec((bn, d), lambda i, j: (j, 0)),
                  pl.BlockSpec((bn, d), lambda i, j: (j + nb, 0)),
                  pl.BlockSpec((bn, d), lambda i, j: (j + 2 * nb, 0)),
                  pl.BlockSpec((None, NA_WIDTH, bn), lambda i, j: (layer, 0, j)),
                  pl.BlockSpec((None, POOL_WIDTH, bn), lambda i, j: (layer, NA_WIDTH // POOL_WIDTH, j)),
                  pl.BlockSpec((None, ML_WIDTH, bn), lambda i, j: (layer, (NA_WIDTH + POOL_WIDTH) // ML_WIDTH, j))],
        out_specs=pl.BlockSpec((bm, bn), lambda i, j: (i, j)),
        compiler_params=_params(("parallel", "arbitrary")),
        name="mm_merge",
    )(h, na, po, ml, wg, wg, wg, w_branch, w_branch, w_branch)


def _softmax_rows(parts):
    m = parts[0].max(axis=-1, keepdims=True)
    for s in parts[1:]:
        m = jnp.maximum(m, s.max(axis=-1, keepdims=True))
    es = [jnp.exp(s - m) for s in parts]
    den = es[0].sum(axis=-1, keepdims=True)
    for e in es[1:]:
        den = den + e.sum(axis=-1, keepdims=True)
    return [e / den for e in es]


def _ctx_attn_kernel(q_ref, k_ref, v_ref, o_ref):
    scale = NA_HEAD_DIM ** -0.5
    for h in range(NA_HEADS):
        sl = slice(h * NA_HEAD_DIM, (h + 1) * NA_HEAD_DIM)
        q = q_ref[:, sl].astype(BF16)
        k = k_ref[:, sl].astype(BF16)
        v = v_ref[:, sl].astype(BF16)
        (p,) = _softmax_rows([_dot_nt(q, k) * scale])
        o_ref[:, sl] = jnp.dot(p.astype(BF16), v, preferred_element_type=F32).astype(o_ref.dtype)


def _ctx_attention(proj, n_seq, seq):
    blk = lambda c: pl.BlockSpec((seq, NA_WIDTH), lambda b: (b, c // NA_WIDTH))
    return pl.pallas_call(
        _ctx_attn_kernel,
        out_shape=jax.ShapeDtypeStruct((proj.shape[0], NA_WIDTH), BF16),
        grid=(n_seq,),
        in_specs=[blk(COL_NA_Q), blk(COL_NA_K), blk(COL_NA_V)],
        out_specs=pl.BlockSpec((seq, NA_WIDTH), lambda b: (b, 0)),
        compiler_params=_params(("parallel",)),
        name="ctx_attention",
    )(proj, proj, proj)


def _na_attn_kernel(q_ref, kf_ref, vf_ref, ckf_ref, cvf_ref, bias_ref, prev_ref, o_ref,
                    k_ref, v_ref, ck_ref, cv_ref, *, rows):
    del prev_ref
    r = pl.program_id(1)

    @pl.when(r == 0)
    def _():
        k_ref[...] = kf_ref[...].astype(BF16)
        v_ref[...] = vf_ref[...].astype(BF16)
        ck_ref[...] = ckf_ref[...].astype(BF16)
        cv_ref[...] = cvf_ref[...].astype(BF16)

    r0 = r * NA_ROWS_PER_STEP
    u = jnp.clip(r0 - NA_WIN_ROWS // 2, 0, rows - NA_SPAN_ROWS)
    k0 = pl.multiple_of(u * GRID_W, GRID_W)
    n_loc = NA_SPAN_ROWS * GRID_W
    slot = []
    for a in range(NA_ROWS_PER_STEP):
        start = jnp.clip(r0 + a - NA_WIN_ROWS // 2, 0, rows - NA_WIN_ROWS)
        slot.append([jnp.where((u + i >= start) & (u + i < start + NA_WIN_ROWS),
                               u + i - (r0 + a) + (NA_WIN_ROWS - 1), NA_MASKED_SLOT)
                     for i in range(NA_SPAN_ROWS)])
    scale = NA_HEAD_DIM ** -0.5
    for h in range(NA_HEADS):
        sl = slice(h * NA_HEAD_DIM, (h + 1) * NA_HEAD_DIM)
        q = q_ref[:, sl].astype(BF16)
        kl = k_ref[pl.ds(k0, n_loc), sl]
        vl = v_ref[pl.ds(k0, n_loc), sl]
        bias = jnp.concatenate([
            jnp.concatenate([bias_ref[h, 0, slot[a][2 * p]] + bias_ref[h, 1, slot[a][2 * p + 1]]
                             for p in range(NA_SPAN_ROWS // 2)], axis=-1)
            for a in range(NA_ROWS_PER_STEP)], axis=0)
        s_loc = jnp.where(bias > -jnp.inf, _dot_nt(q, kl) * scale + bias, MASK_VALUE)
        s_ctx = _dot_nt(q, ck_ref[:, sl]) * scale
        p_loc, p_ctx = _softmax_rows([s_loc, s_ctx])
        out = (jnp.dot(p_loc.astype(BF16), vl, preferred_element_type=F32)
               + jnp.dot(p_ctx.astype(BF16), cv_ref[:, sl], preferred_element_type=F32))
        o_ref[:, sl] = out.astype(o_ref.dtype)


def _na_attention(proj, row0, n_seq, seq, ctx_k, ctx_v, layer, bias_tab, prev):
    rows = seq // GRID_W
    assert rows % NA_ROWS_PER_STEP == 0 and rows >= NA_SPAN_ROWS
    steps = rows // NA_ROWS_PER_STEP
    bq = NA_ROWS_PER_STEP * GRID_W
    past = ctx_k.shape[2]
    rb = row0 // bq
    q_spec = pl.BlockSpec((bq, NA_WIDTH), lambda b, r: (rb + b * steps + r, COL_NA_Q // NA_WIDTH))
    kv = lambda c: _resident((seq, NA_WIDTH), lambda b, r: (row0 // seq + b, c // NA_WIDTH))
    ctx = _resident((None, None, past, NA_WIDTH), lambda b, r: (b, layer, 0, 0))
    return pl.pallas_call(
        functools.partial(_na_attn_kernel, rows=rows),
        out_shape=jax.ShapeDtypeStruct(prev.shape, prev.dtype),
        grid=(n_seq, steps),
        in_specs=[q_spec, kv(COL_NA_K), kv(COL_NA_V), ctx, ctx,
                  _resident(bias_tab.shape, lambda b, r: (0, 0, 0, 0, 0)), _ANY],
        out_specs=pl.BlockSpec((bq, NA_WIDTH), lambda b, r: (rb + b * steps + r, 0)),
        input_output_aliases={6: 0},
        scratch_shapes=[pltpu.VMEM((seq, NA_WIDTH), BF16), pltpu.VMEM((seq, NA_WIDTH), BF16),
                        pltpu.VMEM((past, NA_WIDTH), BF16), pltpu.VMEM((past, NA_WIDTH), BF16)],
        compiler_params=_params(("arbitrary", "arbitrary")),
        name="na_attention",
    )(proj, proj, proj, ctx_k, ctx_v, bias_tab, prev)


def _rel_bias_table(rel_bias):
    cq = np.arange(GRID_W)
    dc = np.clip(cq[None, :] - cq[:, None], -(NA_WIN_COLS - 1), NA_WIN_COLS - 1) + (NA_WIN_COLS - 1)
    col_start = np.clip(cq - NA_WIN_COLS // 2, 0, GRID_W - NA_WIN_COLS)
    col_mask = (cq[None, :] >= col_start[:, None]) & (cq[None, :] < col_start[:, None] + NA_WIN_COLS)
    tiles = jnp.where(col_mask, rel_bias[:, :, dc], -jnp.inf)
    tiles = jnp.concatenate([tiles, jnp.full((NA_HEADS, 1, GRID_W, GRID_W), -jnp.inf, F32)], axis=1)
    zeros = jnp.zeros_like(tiles)
    return jnp.stack([jnp.concatenate([tiles, zeros], axis=-1), jnp.concatenate([zeros, tiles], axis=-1)], axis=1)


def _pool_kernel(u_ref, w_ref, s_ref, *rest, seq):
    o_ref = rest[-1]
    pos = lax.broadcasted_iota(jnp.int32, (seq, POOL_GROUP), 0)
    for g, win in enumerate(POOL_WINDOWS):
        sl = slice(g * POOL_GROUP, (g + 1) * POOL_GROUP)
        u = u_ref[:, sl]
        acc = jnp.zeros_like(u)
        for d in range(-(win // 2), win - win // 2):
            shifted = u if d == 0 else pltpu.roll(u, (-d) % seq, 0)
            valid = (pos + d >= 0) & (pos + d < seq)
            acc = acc + jnp.where(valid, shifted, 0.0)
        lo = jnp.clip(pos - win // 2, 0, seq)
        hi = jnp.clip(pos + win - win // 2, 0, seq)
        pooled = acc / (hi - lo).astype(F32) - u
        y = jnp.dot(pooled.astype(BF16), w_ref[g].astype(BF16), preferred_element_type=F32)
        o_ref[:, sl] = (y * s_ref[:, sl]).astype(o_ref.dtype)


def _pool(proj, row0, n_seq, seq, pool_w, pool_scale, prev=None):
    in_specs = [pl.BlockSpec((seq, POOL_WIDTH), lambda b: (row0 // seq + b, COL_POOL // POOL_WIDTH)),
                pl.BlockSpec(pool_w.shape, lambda b: (0, 0, 0)),
                pl.BlockSpec((1, POOL_WIDTH), lambda b: (0, 0))]
    args = [proj, pool_w, pool_scale]
    aliases = {}
    if prev is not None:
        in_specs.append(_ANY)
        args.append(prev)
        aliases = {3: 0}
    return pl.pallas_call(
        functools.partial(_pool_kernel, seq=seq),
        out_shape=jax.ShapeDtypeStruct((proj.shape[0], POOL_WIDTH), BF16),
        grid=(n_seq,),
        in_specs=in_specs,
        out_specs=pl.BlockSpec((seq, POOL_WIDTH), lambda b: (row0 // seq + b, 0)),
        input_output_aliases=aliases,
        compiler_params=_params(("parallel",)),
        name="pool",
    )(*args)


def _log_sigmoid(x):
    return jnp.minimum(x, 0.0) - jnp.log1p(jnp.exp(-jnp.abs(x)))


def _scan_cumsum(x, reverse, row_idx):
    n = x.shape[0]
    s = 1
    while s < n:
        if reverse:
            x = x + jnp.where(row_idx < n - s, pltpu.roll(x, n - s, 0), 0.0)
        else:
            x = x + jnp.where(row_idx >= s, pltpu.roll(x, s, 0), 0.0)
        s *= 2
    return x


def _mlstm_chunk(q, k, v, i_col, b_col, r_row, total, c_prev, n_prev, m_prev, vis):
    w = jnp.where(vis, r_row, -jnp.inf)
    inter = b_col + m_prev
    m_t = jnp.maximum(inter, b_col + jnp.max(w, axis=1, keepdims=True))
    w_inter = jnp.exp(inter - m_t)
    ks = k * (ML_HEAD_DIM ** -0.5)
    qb = q.astype(BF16)
    vb = v.astype(BF16)
    qk = _dot_nt(qb, ks.astype(BF16)) * jnp.exp(w + (b_col - m_t))
    num = (jnp.dot(qk.astype(BF16), vb, preferred_element_type=F32)
           + w_inter * jnp.dot(qb, c_prev.astype(BF16), preferred_element_type=F32))
    den = jnp.sum(qk, axis=1, keepdims=True) + w_inter * jnp.sum(q * n_prev, axis=1, keepdims=True)
    h = num * (1.0 / jnp.maximum(jnp.abs(den), jnp.exp(-m_t)))
    g = total - b_col + i_col
    m_new = jnp.maximum(total + m_prev, jnp.max(g, axis=0, keepdims=True))
    ws = jnp.exp(g - m_new)
    decay = jnp.exp(total + m_prev - m_new)
    kw = ks * ws
    c_new = decay * c_prev + lax.dot_general(kw.astype(BF16), vb, (((0,), (0,)), ((), ())),
                                             preferred_element_type=F32)
    n_new = decay * n_prev + jnp.sum(kw, axis=0, keepdims=True)
    return h, c_new, n_new, m_new


def _mlstm_kernel(*refs, has_init, n_prev, emit_state):
    qf, kf, vf, gf, qb, kb, vb, gb = refs[:8]
    pos = 8
    if has_init:
        c0_ref, n0_ref, m0_ref = refs[pos:pos + 3]
        pos += 3
    pos += n_prev
    hf_ref, hb_ref = refs[pos:pos + 2]
    pos += 2
    if emit_state:
        co_ref, no_ref, mo_ref = refs[pos:pos + 3]
        pos += 3
    c_s, n_s, m_s = refs[pos:pos + 3]
    c = pl.program_id(1)

    @pl.when(c == 0)
    def _():
        if has_init:
            c_s[...] = c0_ref[...]
            n_s[...] = n0_ref[...]
            m_s[...] = m0_ref[...]
        else:
            c_s[...] = jnp.zeros_like(c_s)
            n_s[...] = jnp.zeros_like(n_s)
            m_s[...] = jnp.zeros_like(m_s)

    t_idx = lax.broadcasted_iota(jnp.int32, (ML_CHUNK, ML_CHUNK), 0)
    s_idx = lax.broadcasted_iota(jnp.int32, (ML_CHUNK, ML_CHUNK), 1)
    row_idx = lax.broadcasted_iota(jnp.int32, (ML_CHUNK, GATE_PAD), 0)
    for d, (q_ref, k_ref, v_ref, g_ref, h_ref) in enumerate(((qf, kf, vf, gf, hf_ref), (qb, kb, vb, gb, hb_ref))):
        reverse = d == 1
        vis = (s_idx >= t_idx) if reverse else (s_idx <= t_idx)
        gates = g_ref[...]
        cum_f = _scan_cumsum(_log_sigmoid(gates), reverse, row_idx)
        total_row = cum_f[0:1, :] if reverse else cum_f[ML_CHUNK - 1:ML_CHUNK, :]
        gates_t = gates.T
        cum_f_t = cum_f.T
        for h in range(ML_HEADS):
            sl = slice(h * ML_HEAD_DIM, (h + 1) * ML_HEAD_DIM)
            ci = 2 * d * ML_HEADS + h
            cf = (2 * d + 1) * ML_HEADS + h
            out, c_new, n_new, m_new = _mlstm_chunk(
                q_ref[:, sl], k_ref[:, sl], v_ref[:, sl], gates[:, ci:ci + 1], cum_f[:, cf:cf + 1],
                gates_t[ci:ci + 1, :] - cum_f_t[cf:cf + 1, :], total_row[:, cf:cf + 1],
                c_s[d, h], n_s[d, h], m_s[d, h], vis)
            h_ref[:, sl] = out
            c_s[d, h] = c_new
            n_s[d, h] = n_new
            m_s[d, h] = m_new

    if emit_state:
        @pl.when(c == pl.num_programs(1) - 1)
        def _():
            co_ref[...] = c_s[...]
            no_ref[...] = n_s[...]
            mo_ref[...] = m_s[...]


def _mlstm(proj, gates, row0, n_seq, seq, layer, init=None, prev_h=None, emit_state=False, prev_state=None):
    nc = seq // ML_CHUNK
    dh = ML_HEAD_DIM
    rb = row0 // ML_CHUNK
    m_tot = proj.shape[0]

    def chunk_row(b, c, rev):
        return rb + b * nc + ((nc - 1 - c) if rev else c)

    def tok(col, rev):
        return pl.BlockSpec((ML_CHUNK, ML_WIDTH), lambda b, c: (chunk_row(b, c, rev), col // ML_WIDTH))

    def gat(rev):
        return pl.BlockSpec((ML_CHUNK, GATE_PAD), lambda b, c: (chunk_row(b, c, rev), 0))

    def hout(rev):
        return pl.BlockSpec((ML_CHUNK, ML_WIDTH), lambda b, c: (chunk_row(b, c, rev), 0))

    in_specs = [tok(COL_ML_Q, False), tok(COL_ML_K, False), tok(COL_ML_V, False), gat(False),
                tok(COL_ML_Q, True), tok(COL_ML_K, True), tok(COL_ML_V, True), gat(True)]
    args = [proj, proj, proj, gates, proj, proj, proj, gates]
    state_idx = lambda b, c: (b, layer, 0, 0, 0, 0)
    if init is not None:
        c0, n0, m0 = init
        in_specs += [_resident((None, None, 2, ML_HEADS, dh, dh), state_idx),
                     _resident((None, None, 2, ML_HEADS, 1, dh), state_idx),
                     _resident((None, None, 2, ML_HEADS, 1, 1), state_idx)]
        args += [c0, n0, m0]
    aliases = {}
    prevs = list(prev_h or ()) + list(prev_state or ())
    out_base = 0 if prev_h else 2
    for i, p in enumerate(prevs):
        aliases[len(args)] = out_base + i
        in_specs.append(_ANY)
        args.append(p)
    out_shape = [jax.ShapeDtypeStruct((m_tot, ML_WIDTH), F32)] * 2
    out_specs = [hout(False), hout(True)]
    if emit_state:
        out_shape += [jax.ShapeDtypeStruct((n_seq, DEPTH, 2, ML_HEADS, dh, dh), F32),
                      jax.ShapeDtypeStruct((n_seq, DEPTH, 2, ML_HEADS, 1, dh), F32),
                      jax.ShapeDtypeStruct((n_seq, DEPTH, 2, ML_HEADS, 1, 1), F32)]
        out_specs += [pl.BlockSpec((None, None, 2, ML_HEADS, dh, dh), state_idx),
                      pl.BlockSpec((None, None, 2, ML_HEADS, 1, dh), state_idx),
                      pl.BlockSpec((None, None, 2, ML_HEADS, 1, 1), state_idx)]
    return pl.pallas_call(
        functools.partial(_mlstm_kernel, has_init=init is not None, n_prev=len(prevs), emit_state=emit_state),
        out_shape=out_shape,
        grid=(n_seq, nc),
        in_specs=in_specs,
        out_specs=out_specs,
        input_output_aliases=aliases,
        scratch_shapes=[pltpu.VMEM((2, ML_HEADS, dh, dh), F32), pltpu.VMEM((2, ML_HEADS, 1, dh), F32),
                        pltpu.VMEM((2, ML_HEADS, 1, 1), F32)],
        compiler_params=_params(("parallel", "arbitrary")),
        name="mlstm",
    )(*args)


def _ml_post_kernel(hf_ref, hb_ref, o_ref, g_ref, out_ref):
    for h in range(ML_HEADS):
        sl = slice(h * ML_HEAD_DIM, (h + 1) * ML_HEAD_DIM)
        x = hf_ref[:, sl] + hb_ref[:, sl]
        x = x * lax.rsqrt(jnp.mean(x * x, axis=-1, keepdims=True) + EPS)
        x = x * g_ref[:, sl]
        out_ref[:, sl] = (jax.nn.sigmoid(o_ref[:, sl]) * x).astype(out_ref.dtype)


def _ml_post(hf, hb, proj, norm_g, bm=512):
    m = hf.shape[0]
    row = lambda i: (i, 0)
    return pl.pallas_call(
        _ml_post_kernel,
        out_shape=jax.ShapeDtypeStruct((m, ML_WIDTH), BF16),
        grid=(m // bm,),
        in_specs=[pl.BlockSpec((bm, ML_WIDTH), row), pl.BlockSpec((bm, ML_WIDTH), row),
                  pl.BlockSpec((bm, ML_WIDTH), lambda i: (i, COL_ML_O // ML_WIDTH)),
                  pl.BlockSpec((1, ML_WIDTH), lambda i: (0, 0))],
        out_specs=pl.BlockSpec((bm, ML_WIDTH), row),
        compiler_params=_params(("parallel",)),
        name="ml_post",
    )(hf, hb, proj, norm_g)


def kernel(x_prompt, x_sample, c, cache_na_k, cache_na_v, state_mlstm_C, state_mlstm_n, state_mlstm_m, c_ctx, w_ada, b_ada, norm_g, ffn_w_in, ffn_w_out, w_in, na_rel_bias, pool_w, pool_scale, ml_gate_bias, ml_norm_g, w_branch, w_out, final_norm_g):
    batch, seq, d = x_prompt.shape
    dec_batch, dec_seq, _ = x_sample.shape
    past = cache_na_k.shape[2]
    assert d == D_MODEL and dec_batch + 1 <= MOD_ROWS
    m_ctx = batch * seq
    m_lat = dec_batch * dec_seq
    groups = _Groups(m_ctx, dec_seq)

    x = jnp.concatenate([x_prompt.reshape(m_ctx, d), x_sample.reshape(m_lat, d)], axis=0)
    cond = jnp.concatenate([c_ctx[None], c, jnp.zeros((MOD_ROWS - 1 - dec_batch, d), F32)], axis=0)
    mod_all = _modulation(cond, w_ada, b_ada.reshape(DEPTH, 1, N_MOD * d))
    mod_all = mod_all.reshape(DEPTH, MOD_ROWS, N_MOD, d)

    ctx_k = cache_na_k.reshape(dec_batch, DEPTH, past, NA_WIDTH)
    ctx_v = cache_na_v.reshape(dec_batch, DEPTH, past, NA_WIDTH)
    init = (state_mlstm_C,
            state_mlstm_n.reshape(dec_batch, DEPTH, 2, ML_HEADS, 1, ML_HEAD_DIM),
            state_mlstm_m.reshape(dec_batch, DEPTH, 2, ML_HEADS, 1, 1))
    w_in_t = jnp.swapaxes(w_in, 1, 2)

    ks_out, vs_out = [], []
    states = None
    for l in range(DEPTH):
        mod = mod_all[l]
        b_gate = jnp.pad(ml_gate_bias[l], (0, GATE_PAD - GATE_COLS)).reshape(1, GATE_PAD)

        def ffn(x, j, shift_row):
            h = _norm_mod(x, norm_g[l, 2 * j].reshape(1, d), mod, groups, shift_row)
            act, w_o = _mm_swiglu(h, ffn_w_in, ffn_w_out, l, j)
            return _mm_resid(act, w_o, (), x, mod, groups, shift_row + 2, MACARON_W, bm=1024, bn=256)

        x = ffn(x, 0, 0)

        h = _norm_mod(x, norm_g[l, 1].reshape(1, d), mod, groups, 3)
        proj, w_merge = _mm_proj(h, w_in_t, l)
        gates = _mm_gates(h, w_in_t, l, b_gate)

        na_out = _ctx_attention(proj, batch, seq)
        na_out = _na_attention(proj, m_ctx, dec_batch, dec_seq, ctx_k, ctx_v, l, _rel_bias_table(na_rel_bias[l]),
                               na_out)
        ps = pool_scale[l].reshape(1, POOL_WIDTH)
        pool_out = _pool(proj, 0, batch, seq, pool_w[l], ps)
        pool_out = _pool(proj, m_ctx, dec_batch, dec_seq, pool_w[l], ps, prev=pool_out)
        hf, hb, *states = _mlstm(proj, gates, 0, batch, seq, l, emit_state=True, prev_state=states)
        hf, hb = _mlstm(proj, gates, m_ctx, dec_batch, dec_seq, l, init=init, prev_h=(hf, hb))
        ml_out = _ml_post(hf, hb, proj, ml_norm_g[l].reshape(1, ML_WIDTH))

        merged = _mm_merge(h, na_out, pool_out, ml_out, w_merge, w_branch, l)
        x = _mm_resid(merged, w_out, (l,), x, mod, groups, 5, 1.0, bm=1024, bn=512)

        x = ffn(x, 1, 6)

        ks_out.append(proj[:m_ctx, COL_NA_K:COL_NA_K + NA_WIDTH].reshape(batch, seq, NA_HEADS, NA_HEAD_DIM))
        vs_out.append(proj[:m_ctx, COL_NA_V:COL_NA_V + NA_WIDTH].reshape(batch, seq, NA_HEADS, NA_HEAD_DIM))

    g_fin = final_norm_g.reshape(1, d)
    y_prompt = _final_norm(x, 0, m_ctx, g_fin).reshape(batch, seq, d)
    y_sample = _final_norm(x, m_ctx, m_lat, g_fin).reshape(dec_batch, dec_seq, d)
    c_fin, n_fin, m_fin = states
    return (y_prompt, y_sample, jnp.stack(ks_out, axis=1), jnp.stack(vs_out, axis=1),
            c_fin, n_fin.reshape(batch, DEPTH, 2, ML_HEADS, ML_HEAD_DIM), m_fin.reshape(batch, DEPTH, 2, ML_HEADS))
```

```python
import functools

import jax
import jax.numpy as jnp
import numpy as np
from jax import lax
from jax.experimental import pallas as pl
from jax.experimental.pallas import tpu as pltpu

F32 = jnp.float32
BF16 = jnp.bfloat16

D_MODEL = 4096
DEPTH = 2
GRID_W = 64
NA_HEADS = 8
NA_WIDTH = D_MODEL // 4
NA_HEAD_DIM = NA_WIDTH // NA_HEADS
NA_WIN_ROWS = 8
NA_WIN_COLS = 16
NA_ROWS_PER_STEP = 4
NA_SPAN_ROWS = NA_WIN_ROWS + NA_ROWS_PER_STEP
NA_MASKED_SLOT = 2 * NA_WIN_ROWS - 1
POOL_WINDOWS = (2, 4, 8, 16)
POOL_WIDTH = D_MODEL // 4
POOL_GROUP = POOL_WIDTH // 4
ML_HEADS = 8
ML_WIDTH = D_MODEL // 2
ML_HEAD_DIM = ML_WIDTH // ML_HEADS
ML_CHUNK = 256
D_FF = 256 * ((8 * D_MODEL // 3 + 255) // 256)
N_MOD = 9
MACARON_W = 0.5
EPS = 1e-6
MASK_VALUE = -1e30

COL_NA_Q = 0
COL_NA_K = NA_WIDTH
COL_NA_V = 2 * NA_WIDTH
COL_POOL = 3 * NA_WIDTH
COL_ML_Q = COL_POOL + POOL_WIDTH
COL_ML_K = COL_ML_Q + ML_WIDTH
COL_ML_V = COL_ML_K + ML_WIDTH
COL_ML_O = COL_ML_V + ML_WIDTH
MAIN_COLS = COL_ML_O + ML_WIDTH
GATE_COLS = 4 * ML_HEADS
GATE_PAD = 128

EPILOGUE_ROW_GROUPS = 2
MOD_ROWS = 8
VMEM_LIMIT = 56 * 1024 * 1024


def _params(sem, vmem=VMEM_LIMIT):
    return pltpu.CompilerParams(dimension_semantics=sem, vmem_limit_bytes=vmem)


def _resident(block_shape, index_map):
    return pl.BlockSpec(block_shape, index_map, pipeline_mode=pl.Buffered(1))


DOUBLE_BUFFER_MAX_BYTES = 8 * 1024 * 1024


def _outer_block(block_shape, index_map, itemsize):
    nbytes = itemsize * int(np.prod([d for d in block_shape if d is not None]))
    if nbytes > DOUBLE_BUFFER_MAX_BYTES:
        return _resident(block_shape, index_map)
    return pl.BlockSpec(block_shape, index_map)


_ANY = pl.BlockSpec(memory_space=pl.ANY)


def _modulation_kernel(c_ref, w_ref, b_ref, o_ref):
    c = c_ref[...]
    s = (c * jax.nn.sigmoid(c)).astype(BF16)
    o_ref[...] = jnp.dot(s, w_ref[...].astype(BF16), preferred_element_type=F32) + b_ref[...]


def _modulation(cond, w_ada, b_ada, bn=512):
    n = w_ada.shape[-1]
    return pl.pallas_call(
        _modulation_kernel,
        out_shape=jax.ShapeDtypeStruct((DEPTH, MOD_ROWS, n), F32),
        grid=(DEPTH, n // bn),
        in_specs=[
            pl.BlockSpec((MOD_ROWS, D_MODEL), lambda l, j: (0, 0)),
            pl.BlockSpec((None, D_MODEL, bn), lambda l, j: (l, 0, j)),
            pl.BlockSpec((None, 1, bn), lambda l, j: (l, 0, j)),
        ],
        out_specs=pl.BlockSpec((None, MOD_ROWS, bn), lambda l, j: (l, 0, j)),
        compiler_params=_params(("parallel", "parallel")),
        name="modulation",
    )(cond, w_ada, b_ada)


def _norm_mod_kernel(x_ref, g_ref, mod_ref, o_ref, *, shift_row):
    x = x_ref[...]
    y = x * lax.rsqrt(jnp.mean(x * x, axis=-1, keepdims=True) + EPS)
    y = y * g_ref[...]
    shift = mod_ref[shift_row:shift_row + 1, :]
    scale = mod_ref[shift_row + 1:shift_row + 2, :]
    o_ref[...] = (y * (1 + scale) + shift).astype(o_ref.dtype)


def _final_norm_kernel(x_ref, g_ref, o_ref):
    x = x_ref[...]
    y = x * lax.rsqrt(jnp.mean(x * x, axis=-1, keepdims=True) + EPS)
    o_ref[...] = y * g_ref[...]


class _Groups:
    def __init__(self, m_ctx, dec_seq):
        self.m_ctx = m_ctx
        self.dec_seq = dec_seq

    def of_block(self, i, bm):
        assert self.m_ctx % bm == 0 and self.dec_seq % bm == 0
        return jnp.maximum((i * bm) // self.dec_seq - (self.m_ctx // self.dec_seq - 1), 0)


def _norm_mod(x, g, mod, groups, shift_row, bm=512):
    m, d = x.shape
    return pl.pallas_call(
        functools.partial(_norm_mod_kernel, shift_row=shift_row),
        out_shape=jax.ShapeDtypeStruct((m, d), BF16),
        grid=(m // bm,),
        in_specs=[
            pl.BlockSpec((bm, d), lambda i: (i, 0)),
            pl.BlockSpec((1, d), lambda i: (0, 0)),
            pl.BlockSpec((None, N_MOD, d), lambda i: (groups.of_block(i, bm), 0, 0)),
        ],
        out_specs=pl.BlockSpec((bm, d), lambda i: (i, 0)),
        compiler_params=_params(("parallel",)),
        name="norm_mod",
    )(x, g, mod)


def _final_norm(x, row0, rows, g, bm=512):
    d = x.shape[1]
    return pl.pallas_call(
        _final_norm_kernel,
        out_shape=jax.ShapeDtypeStruct((rows, d), F32),
        grid=(rows // bm,),
        in_specs=[pl.BlockSpec((bm, d), lambda i: (row0 // bm + i, 0)), pl.BlockSpec((1, d), lambda i: (0, 0))],
        out_specs=pl.BlockSpec((bm, d), lambda i: (i, 0)),
        compiler_params=_params(("parallel",)),
        name="final_norm",
    )(x, g)


def _dot_nt(a, b):
    return lax.dot_general(a, b, (((1,), (1,)), ((), ())), preferred_element_type=F32)


def _mm_proj_kernel(h_ref, wt_ref, *rest):
    *wg_parts, o_ref, wg_bf16_ref = rest
    part = wg_parts[0].shape[0]
    for p, wg_ref in enumerate(wg_parts):
        wg_bf16_ref[p * part:(p + 1) * part, :] = wg_ref[...].astype(BF16)
    o_ref[...] = _dot_nt(h_ref[...], wt_ref[...].astype(BF16))


def _mm_bias_kernel(h_ref, wt_ref, b_ref, o_ref):
    o_ref[...] = _dot_nt(h_ref[...], wt_ref[...].astype(BF16)) + b_ref[...]


def _mm_swiglu_kernel(h_ref, wa_ref, wb_ref, wo_ref, o_ref, wo_bf16_ref):
    wo_bf16_ref[...] = wo_ref[...].astype(BF16)
    wa = wa_ref[...].astype(BF16)
    wb = wb_ref[...].astype(BF16)
    half = h_ref.shape[0] // EPILOGUE_ROW_GROUPS
    for r in range(EPILOGUE_ROW_GROUPS):
        rows = slice(r * half, (r + 1) * half)
        h = h_ref[rows, :]
        a = jnp.dot(h, wa, preferred_element_type=F32)
        b = jnp.dot(h, wb, preferred_element_type=F32)
        o_ref[rows, :] = ((a * jax.nn.sigmoid(a)) * b).astype(o_ref.dtype)


def _mm_resid_kernel(a_ref, w_ref, x_ref, mod_ref, o_ref, *, gate_row, coef):
    y = jnp.dot(a_ref[...], w_ref[...].astype(BF16), preferred_element_type=F32)
    gate = mod_ref[gate_row:gate_row + 1, :]
    o_ref[...] = x_ref[...] + (coef * gate) * y


def _mm_merge_kernel(h_ref, na_ref, po_ref, ml_ref, wg_na, wg_po, wg_ml, wb_na, wb_po, wb_ml, o_ref):
    h = h_ref[...]

    def branch(x_ref, wg_ref, wb_ref):
        g = _dot_nt(h, wg_ref[...])
        y = jnp.dot(x_ref[...], wb_ref[...].astype(BF16), preferred_element_type=F32)
        return jax.nn.sigmoid(g) * y

    o = branch(na_ref, wg_na, wb_na) + branch(po_ref, wg_po, wb_po) + branch(ml_ref, wg_ml, wb_ml)
    o_ref[...] = o.astype(o_ref.dtype)


def _mm_proj(h, w_in_t, layer, bm=2048, bn=512):
    m, k = h.shape
    nb = MAIN_COLS // bn
    n_merge = w_in_t.shape[1] - MAIN_COLS - GATE_COLS
    slab = n_merge // ((m // bm) * nb)
    part = GATE_COLS
    parts = slab // part
    assert slab * (m // bm) * nb == n_merge and parts * part == slab and (MAIN_COLS + GATE_COLS) % part == 0
    first = (MAIN_COLS + GATE_COLS) // part

    def part_spec(p):
        return pl.BlockSpec((None, part, k), lambda i, j: (layer, first + parts * (i * nb + j) + p, 0))

    return pl.pallas_call(
        _mm_proj_kernel,
        out_shape=[jax.ShapeDtypeStruct((m, MAIN_COLS), F32), jax.ShapeDtypeStruct((n_merge, k), BF16)],
        grid=(m // bm, nb),
        in_specs=[_outer_block((bm, k), lambda i, j: (i, 0), h.dtype.itemsize),
                  pl.BlockSpec((None, bn, k), lambda i, j: (layer, j, 0))] + [part_spec(p) for p in range(parts)],
        out_specs=[pl.BlockSpec((bm, bn), lambda i, j: (i, j)),
                   pl.BlockSpec((slab, k), lambda i, j: (i * nb + j, 0))],
        compiler_params=_params(("parallel", "arbitrary")),
        name="mm_proj",
    )(h, w_in_t, *([w_in_t] * parts))


def _mm_gates(h, w_in_t, layer, b, bm=1024):
    m, k = h.shape
    assert MAIN_COLS % GATE_PAD == 0
    return pl.pallas_call(
        _mm_bias_kernel,
        out_shape=jax.ShapeDtypeStruct((m, GATE_PAD), F32),
        grid=(m // bm,),
        in_specs=[pl.BlockSpec((bm, k), lambda i: (i, 0)),
                  pl.BlockSpec((None, GATE_PAD, k), lambda i: (layer, MAIN_COLS // GATE_PAD, 0)),
                  pl.BlockSpec((1, GATE_PAD), lambda i: (0, 0))],
        out_specs=pl.BlockSpec((bm, GATE_PAD), lambda i: (i, 0)),
        compiler_params=_params(("parallel",)),
        name="mm_gates",
    )(h, w_in_t, b)


def _mm_swiglu(h, ffn_w_in, ffn_w_out, layer, j, bm=2048, bn=256):
    m, k = h.shape
    nb = D_FF // bn
    steps = (m // bm) * nb
    slab = D_FF // steps
    assert slab * steps == D_FF and slab % 16 == 0
    d_out = ffn_w_out.shape[-1]
    return pl.pallas_call(
        _mm_swiglu_kernel,
        out_shape=[jax.ShapeDtypeStruct((m, D_FF), BF16), jax.ShapeDtypeStruct((D_FF, d_out), BF16)],
        grid=(m // bm, nb),
        in_specs=[_resident((bm, k), lambda i, n: (i, 0)),
                  pl.BlockSpec((None, None, k, bn), lambda i, n: (layer, j, 0, n)),
                  pl.BlockSpec((None, None, k, bn), lambda i, n: (layer, j, 0, n + nb)),
                  pl.BlockSpec((None, None, slab, d_out), lambda i, n: (layer, j, i * nb + n, 0))],
        out_specs=[pl.BlockSpec((bm, bn), lambda i, n: (i, n)),
                   pl.BlockSpec((slab, d_out), lambda i, n: (i * nb + n, 0))],
        compiler_params=_params(("parallel", "arbitrary")),
        name="mm_swiglu",
    )(h, ffn_w_in, ffn_w_in, ffn_w_out)


def _mm_resid(a, w, w_index, x, mod, groups, gate_row, coef, bm, bn):
    m, k = a.shape
    n = w.shape[-1]
    lead = (None,) * len(w_index)
    return pl.pallas_call(
        functools.partial(_mm_resid_kernel, gate_row=gate_row, coef=coef),
        out_shape=jax.ShapeDtypeStruct((m, n), F32),
        grid=(m // bm, n // bn),
        in_specs=[_outer_block((bm, k), lambda i, j: (i, 0), a.dtype.itemsize),
                  pl.BlockSpec(lead + (k, bn), lambda i, j: w_index + (0, j)),
                  pl.BlockSpec((bm, bn), lambda i, j: (i, j)),
                  pl.BlockSpec((None, N_MOD, bn), lambda i, j: (groups.of_block(i, bm), 0, j))],
        out_specs=pl.BlockSpec((bm, bn), lambda i, j: (i, j)),
        compiler_params=_params(("parallel", "arbitrary")),
        name="mm_resid",
    )(a, w, x, mod)


def _mm_merge(h, na, po, ml, wg, w_branch, layer, bm=1024, bn=256):
    m, d = h.shape
    nb = d // bn
    row = lambda i, j: (i, 0)
    return pl.pallas_call(
        _mm_merge_kernel,
        out_shape=jax.ShapeDtypeStruct((m, d), BF16),
        grid=(m // bm, nb),
        in_specs=[_resident((bm, d), row),
                  _resident((bm, NA_WIDTH), row),
                  _resident((bm, POOL_WIDTH), row),
                  _resident((bm, ML_WIDTH), row),
                  pl.BlockSpec((bn, d), lambda i, j: (j, 0)),
                  pl.BlockSpec((bn, d), lambda i, j: (j + nb, 0)),
                  pl.BlockSpec((bn, d), lambda i, j: (j + 2 * nb, 0)),
                  pl.BlockSpec((None, NA_WIDTH, bn), lambda i, j: (layer, 0, j)),
                  pl.BlockSpec((None, POOL_WIDTH, bn), lambda i, j: (layer, NA_WIDTH // POOL_WIDTH, j)),
                  pl.BlockSpec((None, ML_WIDTH, bn), lambda i, j: (layer, (NA_WIDTH + POOL_WIDTH) // ML_WIDTH, j))],
        out_specs=pl.BlockSpec((bm, bn), lambda i, j: (i, j)),
        compiler_params=_params(("parallel", "arbitrary")),
        name="mm_merge",
    )(h, na, po, ml, wg, wg, wg, w_branch, w_branch, w_branch)


def _softmax_rows(parts):
    m = parts[0].max(axis=-1, keepdims=True)
    for s in parts[1:]:
        m = jnp.maximum(m, s.max(axis=-1, keepdims=True))
    es = [jnp.exp(s - m) for s in parts]
    den = es[0].sum(axis=-1, keepdims=True)
    for e in es[1:]:
        den = den + e.sum(axis=-1, keepdims=True)
    inv = 1.0 / den
    return [e * inv for e in es]


def _ctx_attn_kernel(q_ref, k_ref, v_ref, o_ref):
    scale = NA_HEAD_DIM ** -0.5
    for h in range(NA_HEADS):
        sl = slice(h * NA_HEAD_DIM, (h + 1) * NA_HEAD_DIM)
        q = q_ref[:, sl].astype(BF16)
        k = k_ref[:, sl].astype(BF16)
        v = v_ref[:, sl].astype(BF16)
        (p,) = _softmax_rows([_dot_nt(q, k) * scale])
        o_ref[:, sl] = jnp.dot(p.astype(BF16), v, preferred_element_type=F32).astype(o_ref.dtype)


def _ctx_attention(proj, n_seq, seq):
    blk = lambda c: pl.BlockSpec((seq, NA_WIDTH), lambda b: (b, c // NA_WIDTH))
    return pl.pallas_call(
        _ctx_attn_kernel,
        out_shape=jax.ShapeDtypeStruct((proj.shape[0], NA_WIDTH), BF16),
        grid=(n_seq,),
        in_specs=[blk(COL_NA_Q), blk(COL_NA_K), blk(COL_NA_V)],
        out_specs=pl.BlockSpec((seq, NA_WIDTH), lambda b: (b, 0)),
        compiler_params=_params(("parallel",)),
        name="ctx_attention",
    )(proj, proj, proj)


def _na_attn_kernel(q_ref, kf_ref, vf_ref, ckf_ref, cvf_ref, bias_ref, prev_ref, o_ref,
                    k_ref, v_ref, ck_ref, cv_ref, *, rows):
    del prev_ref
    r = pl.program_id(1)

    @pl.when(r == 0)
    def _():
        k_ref[...] = kf_ref[...].astype(BF16)
        v_ref[...] = vf_ref[...].astype(BF16)
        ck_ref[...] = ckf_ref[...].astype(BF16)
        cv_ref[...] = cvf_ref[...].astype(BF16)

    r0 = r * NA_ROWS_PER_STEP
    u = jnp.clip(r0 - NA_WIN_ROWS // 2, 0, rows - NA_SPAN_ROWS)
    k0 = pl.multiple_of(u * GRID_W, GRID_W)
    n_loc = NA_SPAN_ROWS * GRID_W
    slot = []
    for a in range(NA_ROWS_PER_STEP):
        start = jnp.clip(r0 + a - NA_WIN_ROWS // 2, 0, rows - NA_WIN_ROWS)
        slot.append([jnp.where((u + i >= start) & (u + i < start + NA_WIN_ROWS),
                               u + i - (r0 + a) + (NA_WIN_ROWS - 1), NA_MASKED_SLOT)
                     for i in range(NA_SPAN_ROWS)])
    scale = NA_HEAD_DIM ** -0.5
    for h in range(NA_HEADS):
        sl = slice(h * NA_HEAD_DIM, (h + 1) * NA_HEAD_DIM)
        q = q_ref[:, sl].astype(BF16)
        kl = k_ref[pl.ds(k0, n_loc), sl]
        vl = v_ref[pl.ds(k0, n_loc), sl]
        bias = jnp.concatenate([
            jnp.concatenate([bias_ref[h, 0, slot[a][2 * p]] + bias_ref[h, 1, slot[a][2 * p + 1]]
                             for p in range(NA_SPAN_ROWS // 2)], axis=-1)
            for a in range(NA_ROWS_PER_STEP)], axis=0)
        s_loc = jnp.where(bias > -jnp.inf, _dot_nt(q, kl) * scale + bias, MASK_VALUE)
        s_ctx = _dot_nt(q, ck_ref[:, sl]) * scale
        p_loc, p_ctx = _softmax_rows([s_loc, s_ctx])
        out = (jnp.dot(p_loc.astype(BF16), vl, preferred_element_type=F32)
               + jnp.dot(p_ctx.astype(BF16), cv_ref[:, sl], preferred_element_type=F32))
        o_ref[:, sl] = out.astype(o_ref.dtype)


def _na_attention(proj, row0, n_seq, seq, ctx_k, ctx_v, layer, bias_tab, prev):
    rows = seq // GRID_W
    assert rows % NA_ROWS_PER_STEP == 0 and rows >= NA_SPAN_ROWS
    steps = rows // NA_ROWS_PER_STEP
    bq = NA_ROWS_PER_STEP * GRID_W
    past = ctx_k.shape[2]
    rb = row0 // bq
    q_spec = pl.BlockSpec((bq, NA_WIDTH), lambda b, r: (rb + b * steps + r, COL_NA_Q // NA_WIDTH))
    kv = lambda c: _resident((seq, NA_WIDTH), lambda b, r: (row0 // seq + b, c // NA_WIDTH))
    ctx = _resident((None, None, past, NA_WIDTH), lambda b, r: (b, layer, 0, 0))
    return pl.pallas_call(
        functools.partial(_na_attn_kernel, rows=rows),
        out_shape=jax.ShapeDtypeStruct(prev.shape, prev.dtype),
        grid=(n_seq, steps),
        in_specs=[q_spec, kv(COL_NA_K), kv(COL_NA_V), ctx, ctx,
                  _resident(bias_tab.shape, lambda b, r: (0, 0, 0, 0, 0)), _ANY],
        out_specs=pl.BlockSpec((bq, NA_WIDTH), lambda b, r: (rb + b * steps + r, 0)),
        input_output_aliases={6: 0},
        scratch_shapes=[pltpu.VMEM((seq, NA_WIDTH), BF16), pltpu.VMEM((seq, NA_WIDTH), BF16),
                        pltpu.VMEM((past, NA_WIDTH), BF16), pltpu.VMEM((past, NA_WIDTH), BF16)],
        compiler_params=_params(("arbitrary", "arbitrary")),
        name="na_attention",
    )(proj, proj, proj, ctx_k, ctx_v, bias_tab, prev)


def _rel_bias_table(rel_bias):
    cq = np.arange(GRID_W)
    dc = np.clip(cq[None, :] - cq[:, None], -(NA_WIN_COLS - 1), NA_WIN_COLS - 1) + (NA_WIN_COLS - 1)
    col_start = np.clip(cq - NA_WIN_COLS // 2, 0, GRID_W - NA_WIN_COLS)
    col_mask = (cq[None, :] >= col_start[:, None]) & (cq[None, :] < col_start[:, None] + NA_WIN_COLS)
    tiles = jnp.where(col_mask, rel_bias[:, :, dc], -jnp.inf)
    tiles = jnp.concatenate([tiles, jnp.full((NA_HEADS, 1, GRID_W, GRID_W), -jnp.inf, F32)], axis=1)
    zeros = jnp.zeros_like(tiles)
    return jnp.stack([jnp.concatenate([tiles, zeros], axis=-1), jnp.concatenate([zeros, tiles], axis=-1)], axis=1)


def _pool_kernel(u_ref, w_ref, s_ref, *rest, seq):
    o_ref = rest[-1]
    pos = lax.broadcasted_iota(jnp.int32, (seq, POOL_GROUP), 0)
    for g, win in enumerate(POOL_WINDOWS):
        sl = slice(g * POOL_GROUP, (g + 1) * POOL_GROUP)
        u = u_ref[:, sl]
        acc = jnp.zeros_like(u)
        for d in range(-(win // 2), win - win // 2):
            shifted = u if d == 0 else pltpu.roll(u, (-d) % seq, 0)
            valid = (pos + d >= 0) & (pos + d < seq)
            acc = acc + jnp.where(valid, shifted, 0.0)
        lo = jnp.clip(pos - win // 2, 0, seq)
        hi = jnp.clip(pos + win - win // 2, 0, seq)
        pooled = acc / (hi - lo).astype(F32) - u
        y = jnp.dot(pooled.astype(BF16), w_ref[g].astype(BF16), preferred_element_type=F32)
        o_ref[:, sl] = (y * s_ref[:, sl]).astype(o_ref.dtype)


def _pool(proj, row0, n_seq, seq, pool_w, pool_scale, prev=None):
    in_specs = [pl.BlockSpec((seq, POOL_WIDTH), lambda b: (row0 // seq + b, COL_POOL // POOL_WIDTH)),
                pl.BlockSpec(pool_w.shape, lambda b: (0, 0, 0)),
                pl.BlockSpec((1, POOL_WIDTH), lambda b: (0, 0))]
    args = [proj, pool_w, pool_scale]
    aliases = {}
    if prev is not None:
        in_specs.append(_ANY)
        args.append(prev)
        aliases = {3: 0}
    return pl.pallas_call(
        functools.partial(_pool_kernel, seq=seq),
        out_shape=jax.ShapeDtypeStruct((proj.shape[0], POOL_WIDTH), BF16),
        grid=(n_seq,),
        in_specs=in_specs,
        out_specs=pl.BlockSpec((seq, POOL_WIDTH), lambda b: (row0 // seq + b, 0)),
        input_output_aliases=aliases,
        compiler_params=_params(("parallel",)),
        name="pool",
    )(*args)


def _log_sigmoid(x):
    return jnp.minimum(x, 0.0) - jnp.log1p(jnp.exp(-jnp.abs(x)))


def _scan_cumsum(x, reverse, row_idx):
    n = x.shape[0]
    s = 1
    while s < n:
        if reverse:
            x = x + jnp.where(row_idx < n - s, pltpu.roll(x, n - s, 0), 0.0)
        else:
            x = x + jnp.where(row_idx >= s, pltpu.roll(x, s, 0), 0.0)
        s *= 2
    return x


def _mlstm_chunk(q, k, v, i_col, b_col, r_row, total, c_prev, n_prev, m_prev, vis):
    w = jnp.where(vis, r_row, -jnp.inf)
    inter = b_col + m_prev
    m_t = jnp.maximum(inter, b_col + jnp.max(w, axis=1, keepdims=True))
    w_inter = jnp.exp(inter - m_t)
    ks = k * (ML_HEAD_DIM ** -0.5)
    qb = q.astype(BF16)
    vb = v.astype(BF16)
    qk = _dot_nt(qb, ks.astype(BF16)) * jnp.exp(w + (b_col - m_t))
    num = (jnp.dot(qk.astype(BF16), vb, preferred_element_type=F32)
           + w_inter * jnp.dot(qb, c_prev.astype(BF16), preferred_element_type=F32))
    den = jnp.sum(qk, axis=1, keepdims=True) + w_inter * jnp.sum(q * n_prev, axis=1, keepdims=True)
    h = num * (1.0 / jnp.maximum(jnp.abs(den), jnp.exp(-m_t)))
    g = total - b_col + i_col
    m_new = jnp.maximum(total + m_prev, jnp.max(g, axis=0, keepdims=True))
    ws = jnp.exp(g - m_new)
    decay = jnp.exp(total + m_prev - m_new)
    kw = ks * ws
    c_new = decay * c_prev + lax.dot_general(kw.astype(BF16), vb, (((0,), (0,)), ((), ())),
                                             preferred_element_type=F32)
    n_new = decay * n_prev + jnp.sum(kw, axis=0, keepdims=True)
    return h, c_new, n_new, m_new


def _mlstm_kernel(*refs, has_init, n_prev, emit_state):
    qf, kf, vf, gf, qb, kb, vb, gb = refs[:8]
    pos = 8
    if has_init:
        c0_ref, n0_ref, m0_ref = refs[pos:pos + 3]
        pos += 3
    pos += n_prev
    hf_ref, hb_ref = refs[pos:pos + 2]
    pos += 2
    if emit_state:
        co_ref, no_ref, mo_ref = refs[pos:pos + 3]
        pos += 3
    c_s, n_s, m_s = refs[pos:pos + 3]
    c = pl.program_id(1)

    @pl.when(c == 0)
    def _():
        if has_init:
            c_s[...] = c0_ref[...]
            n_s[...] = n0_ref[...]
            m_s[...] = m0_ref[...]
        else:
            c_s[...] = jnp.zeros_like(c_s)
            n_s[...] = jnp.zeros_like(n_s)
            m_s[...] = jnp.zeros_like(m_s)

    t_idx = lax.broadcasted_iota(jnp.int32, (ML_CHUNK, ML_CHUNK), 0)
    s_idx = lax.broadcasted_iota(jnp.int32, (ML_CHUNK, ML_CHUNK), 1)
    row_idx = lax.broadcasted_iota(jnp.int32, (ML_CHUNK, GATE_PAD), 0)
    for d, (q_ref, k_ref, v_ref, g_ref, h_ref) in enumerate(((qf, kf, vf, gf, hf_ref), (qb, kb, vb, gb, hb_ref))):
        reverse = d == 1
        vis = (s_idx >= t_idx) if reverse else (s_idx <= t_idx)
        gates = g_ref[...]
        cum_f = _scan_cumsum(_log_sigmoid(gates), reverse, row_idx)
        total_row = cum_f[0:1, :] if reverse else cum_f[ML_CHUNK - 1:ML_CHUNK, :]
        gates_t = gates.T
        cum_f_t = cum_f.T
        for h in range(ML_HEADS):
            sl = slice(h * ML_HEAD_DIM, (h + 1) * ML_HEAD_DIM)
            ci = 2 * d * ML_HEADS + h
            cf = (2 * d + 1) * ML_HEADS + h
            out, c_new, n_new, m_new = _mlstm_chunk(
                q_ref[:, sl], k_ref[:, sl], v_ref[:, sl], gates[:, ci:ci + 1], cum_f[:, cf:cf + 1],
                gates_t[ci:ci + 1, :] - cum_f_t[cf:cf + 1, :], total_row[:, cf:cf + 1],
                c_s[d, h], n_s[d, h], m_s[d, h], vis)
            h_ref[:, sl] = out
            c_s[d, h] = c_new
            n_s[d, h] = n_new
            m_s[d, h] = m_new

    if emit_state:
        @pl.when(c == pl.num_programs(1) - 1)
        def _():
            co_ref[...] = c_s[...]
            no_ref[...] = n_s[...]
            mo_ref[...] = m_s[...]


def _mlstm(proj, gates, row0, n_seq, seq, layer, init=None, prev_h=None, emit_state=False, prev_state=None):
    nc = seq // ML_CHUNK
    dh = ML_HEAD_DIM
    rb = row0 // ML_CHUNK
    m_tot = proj.shape[0]

    def chunk_row(b, c, rev):
        return rb + b * nc + ((nc - 1 - c) if rev else c)

    def tok(col, rev):
        return pl.BlockSpec((ML_CHUNK, ML_WIDTH), lambda b, c: (chunk_row(b, c, rev), col // ML_WIDTH))

    def gat(rev):
        return pl.BlockSpec((ML_CHUNK, GATE_PAD), lambda b, c: (chunk_row(b, c, rev), 0))

    def hout(rev):
        return pl.BlockSpec((ML_CHUNK, ML_WIDTH), lambda b, c: (chunk_row(b, c, rev), 0))

    in_specs = [tok(COL_ML_Q, False), tok(COL_ML_K, False), tok(COL_ML_V, False), gat(False),
                tok(COL_ML_Q, True), tok(COL_ML_K, True), tok(COL_ML_V, True), gat(True)]
    args = [proj, proj, proj, gates, proj, proj, proj, gates]
    state_idx = lambda b, c: (b, layer, 0, 0, 0, 0)
    if init is not None:
        c0, n0, m0 = init
        in_specs += [_resident((None, None, 2, ML_HEADS, dh, dh), state_idx),
                     _resident((None, None, 2, ML_HEADS, 1, dh), state_idx),
                     _resident((None, None, 2, ML_HEADS, 1, 1), state_idx)]
        args += [c0, n0, m0]
    aliases = {}
    prevs = list(prev_h or ()) + list(prev_state or ())
    out_base = 0 if prev_h else 2
    for i, p in enumerate(prevs):
        aliases[len(args)] = out_base + i
        in_specs.append(_ANY)
        args.append(p)
    out_shape = [jax.ShapeDtypeStruct((m_tot, ML_WIDTH), F32)] * 2
    out_specs = [hout(False), hout(True)]
    if emit_state:
        out_shape += [jax.ShapeDtypeStruct((n_seq, DEPTH, 2, ML_HEADS, dh, dh), F32),
                      jax.ShapeDtypeStruct((n_seq, DEPTH, 2, ML_HEADS, 1, dh), F32),
                      jax.ShapeDtypeStruct((n_seq, DEPTH, 2, ML_HEADS, 1, 1), F32)]
        out_specs += [pl.BlockSpec((None, None, 2, ML_HEADS, dh, dh), state_idx),
                      pl.BlockSpec((None, None, 2, ML_HEADS, 1, dh), state_idx),
                      pl.BlockSpec((None, None, 2, ML_HEADS, 1, 1), state_idx)]
    return pl.pallas_call(
        functools.partial(_mlstm_kernel, has_init=init is not None, n_prev=len(prevs), emit_state=emit_state),
        out_shape=out_shape,
        grid=(n_seq, nc),
        in_specs=in_specs,
        out_specs=out_specs,
        input_output_aliases=aliases,
        scratch_shapes=[pltpu.VMEM((2, ML_HEADS, dh, dh), F32), pltpu.VMEM((2, ML_HEADS, 1, dh), F32),
                        pltpu.VMEM((2, ML_HEADS, 1, 1), F32)],
        compiler_params=_params(("parallel", "arbitrary")),
        name="mlstm",
    )(*args)


def _ml_post_kernel(hf_ref, hb_ref, o_ref, g_ref, out_ref):
    for h in range(ML_HEADS):
        sl = slice(h * ML_HEAD_DIM, (h + 1) * ML_HEAD_DIM)
        x = hf_ref[:, sl] + hb_ref[:, sl]
        x = x * lax.rsqrt(jnp.mean(x * x, axis=-1, keepdims=True) + EPS)
        x = x * g_ref[:, sl]
        out_ref[:, sl] = (jax.nn.sigmoid(o_ref[:, sl]) * x).astype(out_ref.dtype)


def _ml_post(hf, hb, proj, norm_g, bm=512):
    m = hf.shape[0]
    row = lambda i: (i, 0)
    return pl.pallas_call(
        _ml_post_kernel,
        out_shape=jax.ShapeDtypeStruct((m, ML_WIDTH), BF16),
        grid=(m // bm,),
        in_specs=[pl.BlockSpec((bm, ML_WIDTH), row), pl.BlockSpec((bm, ML_WIDTH), row),
                  pl.BlockSpec((bm, ML_WIDTH), lambda i: (i, COL_ML_O // ML_WIDTH)),
                  pl.BlockSpec((1, ML_WIDTH), lambda i: (0, 0))],
        out_specs=pl.BlockSpec((bm, ML_WIDTH), row),
        compiler_params=_params(("parallel",)),
        name="ml_post",
    )(hf, hb, proj, norm_g)


def kernel(x_prompt, x_sample, c, cache_na_k, cache_na_v, state_mlstm_C, state_mlstm_n, state_mlstm_m, c_ctx, w_ada, b_ada, norm_g, ffn_w_in, ffn_w_out, w_in, na_rel_bias, pool_w, pool_scale, ml_gate_bias, ml_norm_g, w_branch, w_out, final_norm_g):
    batch, seq, d = x_prompt.shape
    dec_batch, dec_seq, _ = x_sample.shape
    past = cache_na_k.shape[2]
    assert d == D_MODEL and dec_batch + 1 <= MOD_ROWS
    m_ctx = batch * seq
    m_lat = dec_batch * dec_seq
    groups = _Groups(m_ctx, dec_seq)

    x = jnp.concatenate([x_prompt.reshape(m_ctx, d), x_sample.reshape(m_lat, d)], axis=0)
    cond = jnp.concatenate([c_ctx[None], c, jnp.zeros((MOD_ROWS - 1 - dec_batch, d), F32)], axis=0)
    mod_all = _modulation(cond, w_ada, b_ada.reshape(DEPTH, 1, N_MOD * d))
    mod_all = mod_all.reshape(DEPTH, MOD_ROWS, N_MOD, d)

    ctx_k = cache_na_k.reshape(dec_batch, DEPTH, past, NA_WIDTH)
    ctx_v = cache_na_v.reshape(dec_batch, DEPTH, past, NA_WIDTH)
    init = (state_mlstm_C,
            state_mlstm_n.reshape(dec_batch, DEPTH, 2, ML_HEADS, 1, ML_HEAD_DIM),
            state_mlstm_m.reshape(dec_batch, DEPTH, 2, ML_HEADS, 1, 1))
    w_in_t = jnp.swapaxes(w_in, 1, 2)

    ks_out, vs_out = [], []
    states = None
    for l in range(DEPTH):
        mod = mod_all[l]
        b_gate = jnp.pad(ml_gate_bias[l], (0, GATE_PAD - GATE_COLS)).reshape(1, GATE_PAD)

        def ffn(x, j, shift_row):
            h = _norm_mod(x, norm_g[l, 2 * j].reshape(1, d), mod, groups, shift_row)
            act, w_o = _mm_swiglu(h, ffn_w_in, ffn_w_out, l, j)
            return _mm_resid(act, w_o, (), x, mod, groups, shift_row + 2, MACARON_W, bm=1024, bn=256)

        x = ffn(x, 0, 0)

        h = _norm_mod(x, norm_g[l, 1].reshape(1, d), mod, groups, 3)
        proj, w_merge = _mm_proj(h, w_in_t, l)
        gates = _mm_gates(h, w_in_t, l, b_gate)

        na_out = _ctx_attention(proj, batch, seq)
        na_out = _na_attention(proj, m_ctx, dec_batch, dec_seq, ctx_k, ctx_v, l, _rel_bias_table(na_rel_bias[l]),
                               na_out)
        ps = pool_scale[l].reshape(1, POOL_WIDTH)
        pool_out = _pool(proj, 0, batch, seq, pool_w[l], ps)
        pool_out = _pool(proj, m_ctx, dec_batch, dec_seq, pool_w[l], ps, prev=pool_out)
        hf, hb, *states = _mlstm(proj, gates, 0, batch, seq, l, emit_state=True, prev_state=states)
        hf, hb = _mlstm(proj, gates, m_ctx, dec_batch, dec_seq, l, init=init, prev_h=(hf, hb))
        ml_out = _ml_post(hf, hb, proj, ml_norm_g[l].reshape(1, ML_WIDTH))

        merged = _mm_merge(h, na_out, pool_out, ml_out, w_merge, w_branch, l)
        x = _mm_resid(merged, w_out, (l,), x, mod, groups, 5, 1.0, bm=1024, bn=512)

        x = ffn(x, 1, 6)

        ks_out.append(proj[:m_ctx, COL_NA_K:COL_NA_K + NA_WIDTH].reshape(batch, seq, NA_HEADS, NA_HEAD_DIM))
        vs_out.append(proj[:m_ctx, COL_NA_V:COL_NA_V + NA_WIDTH].reshape(batch, seq, NA_HEADS, NA_HEAD_DIM))

    g_fin = final_norm_g.reshape(1, d)
    y_prompt = _final_norm(x, 0, m_ctx, g_fin).reshape(batch, seq, d)
    y_sample = _final_norm(x, m_ctx, m_lat, g_fin).reshape(dec_batch, dec_seq, d)
    c_fin, n_fin, m_fin = states
    return (y_prompt, y_sample, jnp.stack(ks_out, axis=1), jnp.stack(vs_out, axis=1),
            c_fin, n_fin.reshape(batch, DEPTH, 2, ML_HEADS, ML_HEAD_DIM), m_fin.reshape(batch, DEPTH, 2, ML_HEADS))
```

```python
import functools

import jax
import jax.numpy as jnp
import numpy as np
from jax import lax
from jax.experimental import pallas as pl
from jax.experimental.pallas import tpu as pltpu

F32 = jnp.float32
BF16 = jnp.bfloat16

D_MODEL = 4096
DEPTH = 2
GRID_W = 64
NA_HEADS = 8
NA_WIDTH = D_MODEL // 4
NA_HEAD_DIM = NA_WIDTH // NA_HEADS
NA_WIN_ROWS = 8
NA_WIN_COLS = 16
NA_ROWS_PER_STEP = 4
NA_SPAN_ROWS = NA_WIN_ROWS + NA_ROWS_PER_STEP
NA_MASKED_SLOT = 2 * NA_WIN_ROWS - 1
POOL_WINDOWS = (2, 4, 8, 16)
POOL_WIDTH = D_MODEL // 4
POOL_GROUP = POOL_WIDTH // 4
ML_HEADS = 8
ML_WIDTH = D_MODEL // 2
ML_HEAD_DIM = ML_WIDTH // ML_HEADS
ML_CHUNK = 256
D_FF = 256 * ((8 * D_MODEL // 3 + 255) // 256)
N_MOD = 9
MACARON_W = 0.5
EPS = 1e-6
MASK_VALUE = -1e30

COL_NA_Q = 0
COL_NA_K = NA_WIDTH
COL_NA_V = 2 * NA_WIDTH
COL_POOL = 3 * NA_WIDTH
COL_ML_Q = COL_POOL + POOL_WIDTH
COL_ML_K = COL_ML_Q + ML_WIDTH
COL_ML_V = COL_ML_K + ML_WIDTH
COL_ML_O = COL_ML_V + ML_WIDTH
MAIN_COLS = COL_ML_O + ML_WIDTH
GATE_COLS = 4 * ML_HEADS
GATE_PAD = 128

EPILOGUE_ROW_GROUPS = 2
MOD_ROWS = 8
VMEM_LIMIT = 56 * 1024 * 1024


def _params(sem, vmem=VMEM_LIMIT):
    return pltpu.CompilerParams(dimension_semantics=sem, vmem_limit_bytes=vmem)


def _resident(block_shape, index_map):
    return pl.BlockSpec(block_shape, index_map, pipeline_mode=pl.Buffered(1))


DOUBLE_BUFFER_MAX_BYTES = 8 * 1024 * 1024


def _outer_block(block_shape, index_map, itemsize):
    nbytes = itemsize * int(np.prod([d for d in block_shape if d is not None]))
    if nbytes > DOUBLE_BUFFER_MAX_BYTES:
        return _resident(block_shape, index_map)
    return pl.BlockSpec(block_shape, index_map)


_ANY = pl.BlockSpec(memory_space=pl.ANY)


def _modulation_kernel(c_ref, w_ref, b_ref, o_ref):
    c = c_ref[...]
    s = (c * jax.nn.sigmoid(c)).astype(BF16)
    o_ref[...] = jnp.dot(s, w_ref[...].astype(BF16), preferred_element_type=F32) + b_ref[...]


def _modulation(cond, w_ada, b_ada, bn=1024):
    n = w_ada.shape[-1]
    return pl.pallas_call(
        _modulation_kernel,
        out_shape=jax.ShapeDtypeStruct((DEPTH, MOD_ROWS, n), F32),
        grid=(DEPTH, n // bn),
        in_specs=[
            pl.BlockSpec((MOD_ROWS, D_MODEL), lambda l, j: (0, 0)),
            pl.BlockSpec((None, D_MODEL, bn), lambda l, j: (l, 0, j)),
            pl.BlockSpec((None, 1, bn), lambda l, j: (l, 0, j)),
        ],
        out_specs=pl.BlockSpec((None, MOD_ROWS, bn), lambda l, j: (l, 0, j)),
        compiler_params=_params(("parallel", "parallel")),
        name="modulation",
    )(cond, w_ada, b_ada)


def _norm_mod_kernel(x_ref, g_ref, mod_ref, *rest, shift_row):
    o_ref = rest[-1]
    x = x_ref[...]
    y = x * lax.rsqrt(jnp.mean(x * x, axis=-1, keepdims=True) + EPS)
    y = y * g_ref[...]
    shift = mod_ref[shift_row:shift_row + 1, :]
    scale = mod_ref[shift_row + 1:shift_row + 2, :]
    o_ref[...] = (y * (1 + scale) + shift).astype(o_ref.dtype)


def _final_norm_kernel(x_ref, g_ref, o_ref):
    x = x_ref[...]
    y = x * lax.rsqrt(jnp.mean(x * x, axis=-1, keepdims=True) + EPS)
    o_ref[...] = y * g_ref[...]


class _Groups:
    def __init__(self, m_ctx, dec_seq):
        self.m_ctx = m_ctx
        self.dec_seq = dec_seq

    def of_block(self, i, bm):
        assert self.m_ctx % bm == 0 and self.dec_seq % bm == 0
        return jnp.maximum((i * bm) // self.dec_seq - (self.m_ctx // self.dec_seq - 1), 0)


def _norm_mod(x, g, mod, groups, shift_row, row0=0, out_rows=None, prev=None, bm=512):
    m, d = x.shape
    out_rows = m if out_rows is None else out_rows
    assert row0 % bm == 0
    b0 = row0 // bm
    in_specs = [pl.BlockSpec((bm, d), lambda i: (i, 0)),
                pl.BlockSpec((1, d), lambda i: (0, 0)),
                pl.BlockSpec((None, N_MOD, d), lambda i: (groups.of_block(b0 + i, bm), 0, 0))]
    args = [x, g, mod]
    aliases = {}
    if prev is not None:
        in_specs.append(_ANY)
        args.append(prev)
        aliases = {3: 0}
    return pl.pallas_call(
        functools.partial(_norm_mod_kernel, shift_row=shift_row),
        out_shape=jax.ShapeDtypeStruct((out_rows, d), BF16),
        grid=(m // bm,),
        in_specs=in_specs,
        out_specs=pl.BlockSpec((bm, d), lambda i: (b0 + i, 0)),
        input_output_aliases=aliases,
        compiler_params=_params(("parallel",)),
        name="norm_mod",
    )(*args)


def _final_norm(x, row0, rows, g, bm=512):
    d = x.shape[1]
    return pl.pallas_call(
        _final_norm_kernel,
        out_shape=jax.ShapeDtypeStruct((rows, d), F32),
        grid=(rows // bm,),
        in_specs=[pl.BlockSpec((bm, d), lambda i: (row0 // bm + i, 0)), pl.BlockSpec((1, d), lambda i: (0, 0))],
        out_specs=pl.BlockSpec((bm, d), lambda i: (i, 0)),
        compiler_params=_params(("parallel",)),
        name="final_norm",
    )(x, g)


def _dot_nt(a, b):
    return lax.dot_general(a, b, (((1,), (1,)), ((), ())), preferred_element_type=F32)


def _mm_proj_kernel(h_ref, wt_ref, *rest):
    *wg_parts, o_ref, wg_bf16_ref = rest
    part = wg_parts[0].shape[0]
    for p, wg_ref in enumerate(wg_parts):
        wg_bf16_ref[p * part:(p + 1) * part, :] = wg_ref[...].astype(BF16)
    o_ref[...] = _dot_nt(h_ref[...], wt_ref[...].astype(BF16))


def _mm_bias_kernel(h_ref, wt_ref, b_ref, o_ref):
    o_ref[...] = _dot_nt(h_ref[...], wt_ref[...].astype(BF16)) + b_ref[...]


def _mm_swiglu_kernel(h_ref, wa_ref, wb_ref, wo_ref, o_ref, wo_bf16_ref):
    wo_bf16_ref[...] = wo_ref[...].astype(BF16)
    wa = wa_ref[...].astype(BF16)
    wb = wb_ref[...].astype(BF16)
    half = h_ref.shape[0] // EPILOGUE_ROW_GROUPS
    for r in range(EPILOGUE_ROW_GROUPS):
        rows = slice(r * half, (r + 1) * half)
        h = h_ref[rows, :]
        a = jnp.dot(h, wa, preferred_element_type=F32)
        b = jnp.dot(h, wb, preferred_element_type=F32)
        o_ref[rows, :] = ((a * jax.nn.sigmoid(a)) * b).astype(o_ref.dtype)


def _mm_resid_kernel(a_ref, w_ref, x_ref, mod_ref, *rest, gate_row, coef):
    o_ref = rest[-1]
    y = jnp.dot(a_ref[...], w_ref[...].astype(BF16), preferred_element_type=F32)
    gate = mod_ref[gate_row:gate_row + 1, :]
    o_ref[...] = x_ref[...] + (coef * gate) * y


def _mm_merge_kernel(h_ref, na_ref, po_ref, ml_ref, wg_na, wg_po, wg_ml, wb_ref, o_ref):
    h = h_ref[...]

    def branch(x_ref, wg_ref, row0):
        wb = wb_ref[row0:row0 + x_ref.shape[1], :].astype(BF16)
        g = _dot_nt(h, wg_ref[...])
        y = jnp.dot(x_ref[...], wb, preferred_element_type=F32)
        return jax.nn.sigmoid(g) * y

    o = (branch(na_ref, wg_na, 0) + branch(po_ref, wg_po, NA_WIDTH)
         + branch(ml_ref, wg_ml, NA_WIDTH + POOL_WIDTH))
    o_ref[...] = o.astype(o_ref.dtype)


def _mm_proj(h, w_in_t, layer, bm=2048, bn=512):
    m, k = h.shape
    nb = MAIN_COLS // bn
    n_merge = w_in_t.shape[1] - MAIN_COLS - GATE_COLS
    slab = n_merge // ((m // bm) * nb)
    part = GATE_COLS
    parts = slab // part
    assert slab * (m // bm) * nb == n_merge and parts * part == slab and (MAIN_COLS + GATE_COLS) % part == 0
    first = (MAIN_COLS + GATE_COLS) // part

    def part_spec(p):
        return pl.BlockSpec((None, part, k), lambda i, j: (layer, first + parts * (i * nb + j) + p, 0))

    return pl.pallas_call(
        _mm_proj_kernel,
        out_shape=[jax.ShapeDtypeStruct((m, MAIN_COLS), F32), jax.ShapeDtypeStruct((n_merge, k), BF16)],
        grid=(m // bm, nb),
        in_specs=[_outer_block((bm, k), lambda i, j: (i, 0), h.dtype.itemsize),
                  pl.BlockSpec((None, bn, k), lambda i, j: (layer, j, 0))] + [part_spec(p) for p in range(parts)],
        out_specs=[pl.BlockSpec((bm, bn), lambda i, j: (i, j)),
                   pl.BlockSpec((slab, k), lambda i, j: (i * nb + j, 0))],
        compiler_params=_params(("parallel", "arbitrary")),
        name="mm_proj",
    )(h, w_in_t, *([w_in_t] * parts))


def _mm_gates(h, w_in_t, layer, b, bm=1024):
    m, k = h.shape
    assert MAIN_COLS % GATE_PAD == 0
    return pl.pallas_call(
        _mm_bias_kernel,
        out_shape=jax.ShapeDtypeStruct((m, GATE_PAD), F32),
        grid=(m // bm,),
        in_specs=[pl.BlockSpec((bm, k), lambda i: (i, 0)),
                  pl.BlockSpec((None, GATE_PAD, k), lambda i: (layer, MAIN_COLS // GATE_PAD, 0)),
                  pl.BlockSpec((1, GATE_PAD), lambda i: (0, 0))],
        out_specs=pl.BlockSpec((bm, GATE_PAD), lambda i: (i, 0)),
        compiler_params=_params(("parallel",)),
        name="mm_gates",
    )(h, w_in_t, b)


def _mm_swiglu(h, ffn_w_in, ffn_w_out, layer, j, bm=2048, bn=256):
    m, k = h.shape
    nb = D_FF // bn
    steps = (m // bm) * nb
    slab = D_FF // steps
    assert slab * steps == D_FF and slab % 16 == 0
    d_out = ffn_w_out.shape[-1]
    return pl.pallas_call(
        _mm_swiglu_kernel,
        out_shape=[jax.ShapeDtypeStruct((m, D_FF), BF16), jax.ShapeDtypeStruct((D_FF, d_out), BF16)],
        grid=(m // bm, nb),
        in_specs=[_resident((bm, k), lambda i, n: (i, 0)),
                  pl.BlockSpec((None, None, k, bn), lambda i, n: (layer, j, 0, n)),
                  pl.BlockSpec((None, None, k, bn), lambda i, n: (layer, j, 0, n + nb)),
                  pl.BlockSpec((None, None, slab, d_out), lambda i, n: (layer, j, i * nb + n, 0))],
        out_specs=[pl.BlockSpec((bm, bn), lambda i, n: (i, n)),
                   pl.BlockSpec((slab, d_out), lambda i, n: (i * nb + n, 0))],
        compiler_params=_params(("parallel", "arbitrary")),
        name="mm_swiglu",
    )(h, ffn_w_in, ffn_w_in, ffn_w_out)


def _mm_resid(a, w, w_index, x, mod, groups, gate_row, coef, bm, bn, row0=0, prev=None):
    m, k = a.shape
    n = w.shape[-1]
    rows = x.shape[0]
    assert row0 % bm == 0 and rows % bm == 0
    b0 = row0 // bm
    lead = (None,) * len(w_index)
    in_specs = [_outer_block((bm, k), lambda i, j: (b0 + i, 0), a.dtype.itemsize),
                pl.BlockSpec(lead + (k, bn), lambda i, j: w_index + (0, j)),
                pl.BlockSpec((bm, bn), lambda i, j: (i, j)),
                pl.BlockSpec((None, N_MOD, bn), lambda i, j: (groups.of_block(b0 + i, bm), 0, j))]
    args = [a, w, x, mod]
    aliases = {}
    if prev is not None:
        in_specs.append(_ANY)
        args.append(prev)
        aliases = {4: 0}
    return pl.pallas_call(
        functools.partial(_mm_resid_kernel, gate_row=gate_row, coef=coef),
        out_shape=jax.ShapeDtypeStruct((m, n), F32),
        grid=(rows // bm, n // bn),
        in_specs=in_specs,
        out_specs=pl.BlockSpec((bm, bn), lambda i, j: (b0 + i, j)),
        input_output_aliases=aliases,
        compiler_params=_params(("parallel", "arbitrary")),
        name="mm_resid",
    )(*args)


def _mm_merge(h, na, po, ml, wg, w_branch, layer, bm=1024, bn=256):
    m, d = h.shape
    nb = d // bn
    row = lambda i, j: (i, 0)
    return pl.pallas_call(
        _mm_merge_kernel,
        out_shape=jax.ShapeDtypeStruct((m, d), BF16),
        grid=(m // bm, nb),
        in_specs=[_resident((bm, d), row),
                  _resident((bm, NA_WIDTH), row),
                  _resident((bm, POOL_WIDTH), row),
                  _resident((bm, ML_WIDTH), row),
                  pl.BlockSpec((bn, d), lambda i, j: (j, 0)),
                  pl.BlockSpec((bn, d), lambda i, j: (j + nb, 0)),
                  pl.BlockSpec((bn, d), lambda i, j: (j + 2 * nb, 0)),
                  pl.BlockSpec((None, NA_WIDTH + POOL_WIDTH + ML_WIDTH, bn), lambda i, j: (layer, 0, j))],
        out_specs=pl.BlockSpec((bm, bn), lambda i, j: (i, j)),
        compiler_params=_params(("parallel", "arbitrary")),
        name="mm_merge",
    )(h, na, po, ml, wg, wg, wg, w_branch)


def _softmax_rows(parts):
    m = parts[0].max(axis=-1, keepdims=True)
    for s in parts[1:]:
        m = jnp.maximum(m, s.max(axis=-1, keepdims=True))
    es = [jnp.exp(s - m) for s in parts]
    den = es[0].sum(axis=-1, keepdims=True)
    for e in es[1:]:
        den = den + e.sum(axis=-1, keepdims=True)
    inv = 1.0 / den
    return [e * inv for e in es]


def _ctx_attn_kernel(q_ref, k_ref, v_ref, o_ref):
    scale = NA_HEAD_DIM ** -0.5
    for h in range(NA_HEADS):
        sl = slice(h * NA_HEAD_DIM, (h + 1) * NA_HEAD_DIM)
        q = q_ref[:, sl].astype(BF16)
        k = k_ref[:, sl].astype(BF16)
        v = v_ref[:, sl].astype(BF16)
        (p,) = _softmax_rows([_dot_nt(q, k) * scale])
        o_ref[:, sl] = jnp.dot(p.astype(BF16), v, preferred_element_type=F32).astype(o_ref.dtype)


def _ctx_attention(proj, n_seq, seq):
    blk = lambda c: pl.BlockSpec((seq, NA_WIDTH), lambda b: (b, c // NA_WIDTH))
    return pl.pallas_call(
        _ctx_attn_kernel,
        out_shape=jax.ShapeDtypeStruct((proj.shape[0], NA_WIDTH), BF16),
        grid=(n_seq,),
        in_specs=[blk(COL_NA_Q), blk(COL_NA_K), blk(COL_NA_V)],
        out_specs=pl.BlockSpec((seq, NA_WIDTH), lambda b: (b, 0)),
        compiler_params=_params(("parallel",)),
        name="ctx_attention",
    )(proj, proj, proj)


def _na_attn_kernel(q_ref, kf_ref, vf_ref, ckf_ref, cvf_ref, bias_ref, prev_ref, o_ref,
                    k_ref, v_ref, ck_ref, cv_ref, *, rows):
    del prev_ref
    r = pl.program_id(1)

    @pl.when(r == 0)
    def _():
        k_ref[...] = kf_ref[...].astype(BF16)
        v_ref[...] = vf_ref[...].astype(BF16)
        ck_ref[...] = ckf_ref[...].astype(BF16)
        cv_ref[...] = cvf_ref[...].astype(BF16)

    r0 = r * NA_ROWS_PER_STEP
    u = jnp.clip(r0 - NA_WIN_ROWS // 2, 0, rows - NA_SPAN_ROWS)
    k0 = pl.multiple_of(u * GRID_W, GRID_W)
    n_loc = NA_SPAN_ROWS * GRID_W
    slot = []
    for a in range(NA_ROWS_PER_STEP):
        start = jnp.clip(r0 + a - NA_WIN_ROWS // 2, 0, rows - NA_WIN_ROWS)
        slot.append([jnp.where((u + i >= start) & (u + i < start + NA_WIN_ROWS),
                               u + i - (r0 + a) + (NA_WIN_ROWS - 1), NA_MASKED_SLOT)
                     for i in range(NA_SPAN_ROWS)])
    scale = NA_HEAD_DIM ** -0.5
    for h in range(NA_HEADS):
        sl = slice(h * NA_HEAD_DIM, (h + 1) * NA_HEAD_DIM)
        q = q_ref[:, sl].astype(BF16)
        kl = k_ref[pl.ds(k0, n_loc), sl]
        vl = v_ref[pl.ds(k0, n_loc), sl]
        bias = jnp.concatenate([
            jnp.concatenate([bias_ref[h, 0, slot[a][2 * p]] + bias_ref[h, 1, slot[a][2 * p + 1]]
                             for p in range(NA_SPAN_ROWS // 2)], axis=-1)
            for a in range(NA_ROWS_PER_STEP)], axis=0)
        s_loc = jnp.where(bias > -jnp.inf, _dot_nt(q, kl) * scale + bias, MASK_VALUE)
        s_ctx = _dot_nt(q, ck_ref[:, sl]) * scale
        p_loc, p_ctx = _softmax_rows([s_loc, s_ctx])
        out = (jnp.dot(p_loc.astype(BF16), vl, preferred_element_type=F32)
               + jnp.dot(p_ctx.astype(BF16), cv_ref[:, sl], preferred_element_type=F32))
        o_ref[:, sl] = out.astype(o_ref.dtype)


def _na_attention(proj, row0, n_seq, seq, ctx_k, ctx_v, layer, bias_tab, prev):
    rows = seq // GRID_W
    assert rows % NA_ROWS_PER_STEP == 0 and rows >= NA_SPAN_ROWS
    steps = rows // NA_ROWS_PER_STEP
    bq = NA_ROWS_PER_STEP * GRID_W
    past = ctx_k.shape[2]
    rb = row0 // bq
    q_spec = pl.BlockSpec((bq, NA_WIDTH), lambda b, r: (rb + b * steps + r, COL_NA_Q // NA_WIDTH))
    kv = lambda c: _resident((seq, NA_WIDTH), lambda b, r: (row0 // seq + b, c // NA_WIDTH))
    ctx = _resident((None, None, past, NA_WIDTH), lambda b, r: (b, layer, 0, 0))
    return pl.pallas_call(
        functools.partial(_na_attn_kernel, rows=rows),
        out_shape=jax.ShapeDtypeStruct(prev.shape, prev.dtype),
        grid=(n_seq, steps),
        in_specs=[q_spec, kv(COL_NA_K), kv(COL_NA_V), ctx, ctx,
                  _resident(bias_tab.shape, lambda b, r: (0, 0, 0, 0, 0)), _ANY],
        out_specs=pl.BlockSpec((bq, NA_WIDTH), lambda b, r: (rb + b * steps + r, 0)),
        input_output_aliases={6: 0},
        scratch_shapes=[pltpu.VMEM((seq, NA_WIDTH), BF16), pltpu.VMEM((seq, NA_WIDTH), BF16),
                        pltpu.VMEM((past, NA_WIDTH), BF16), pltpu.VMEM((past, NA_WIDTH), BF16)],
        compiler_params=_params(("arbitrary", "arbitrary")),
        name="na_attention",
    )(proj, proj, proj, ctx_k, ctx_v, bias_tab, prev)


def _emit_kv_kernel(k_ref, v_ref, *rest):
    ko_ref, vo_ref = rest[-2:]
    ko_ref[...] = k_ref[...]
    vo_ref[...] = v_ref[...]


def _emit_kv(proj, n_seq, seq, layer, prev=None):
    blk = lambda c: pl.BlockSpec((seq, NA_WIDTH), lambda b: (b, c // NA_WIDTH))
    in_specs = [blk(COL_NA_K), blk(COL_NA_V)]
    args = [proj, proj]
    aliases = {}
    if prev is not None:
        in_specs += [_ANY, _ANY]
        args += list(prev)
        aliases = {2: 0, 3: 1}
    out = pl.BlockSpec((None, None, seq, NA_WIDTH), lambda b: (b, layer, 0, 0))
    return pl.pallas_call(
        _emit_kv_kernel,
        out_shape=[jax.ShapeDtypeStruct((n_seq, DEPTH, seq, NA_WIDTH), F32)] * 2,
        grid=(n_seq,),
        in_specs=in_specs,
        out_specs=[out, out],
        input_output_aliases=aliases,
        compiler_params=_params(("parallel",)),
        name="emit_kv",
    )(*args)


def _rel_bias_table(rel_bias):
    cq = np.arange(GRID_W)
    dc = np.clip(cq[None, :] - cq[:, None], -(NA_WIN_COLS - 1), NA_WIN_COLS - 1) + (NA_WIN_COLS - 1)
    col_start = np.clip(cq - NA_WIN_COLS // 2, 0, GRID_W - NA_WIN_COLS)
    col_mask = (cq[None, :] >= col_start[:, None]) & (cq[None, :] < col_start[:, None] + NA_WIN_COLS)
    tiles = jnp.where(col_mask, rel_bias[:, :, dc], -jnp.inf)
    tiles = jnp.concatenate([tiles, jnp.full((NA_HEADS, 1, GRID_W, GRID_W), -jnp.inf, F32)], axis=1)
    zeros = jnp.zeros_like(tiles)
    return jnp.stack([jnp.concatenate([tiles, zeros], axis=-1), jnp.concatenate([zeros, tiles], axis=-1)], axis=1)


def _pool_kernel(u_ref, w_ref, s_ref, *rest, seq):
    o_ref = rest[-1]
    pos = lax.broadcasted_iota(jnp.int32, (seq, POOL_GROUP), 0)
    for g, win in enumerate(POOL_WINDOWS):
        sl = slice(g * POOL_GROUP, (g + 1) * POOL_GROUP)
        u = u_ref[:, sl]
        acc = jnp.zeros_like(u)
        for d in range(-(win // 2), win - win // 2):
            shifted = u if d == 0 else pltpu.roll(u, (-d) % seq, 0)
            valid = (pos + d >= 0) & (pos + d < seq)
            acc = acc + jnp.where(valid, shifted, 0.0)
        lo = jnp.clip(pos - win // 2, 0, seq)
        hi = jnp.clip(pos + win - win // 2, 0, seq)
        pooled = acc / (hi - lo).astype(F32) - u
        y = jnp.dot(pooled.astype(BF16), w_ref[g].astype(BF16), preferred_element_type=F32)
        o_ref[:, sl] = (y * s_ref[:, sl]).astype(o_ref.dtype)


def _pool(proj, row0, n_seq, seq, pool_w, pool_scale, prev=None):
    in_specs = [pl.BlockSpec((seq, POOL_WIDTH), lambda b: (row0 // seq + b, COL_POOL // POOL_WIDTH)),
                pl.BlockSpec(pool_w.shape, lambda b: (0, 0, 0)),
                pl.BlockSpec((1, POOL_WIDTH), lambda b: (0, 0))]
    args = [proj, pool_w, pool_scale]
    aliases = {}
    if prev is not None:
        in_specs.append(_ANY)
        args.append(prev)
        aliases = {3: 0}
    return pl.pallas_call(
        functools.partial(_pool_kernel, seq=seq),
        out_shape=jax.ShapeDtypeStruct((proj.shape[0], POOL_WIDTH), BF16),
        grid=(n_seq,),
        in_specs=in_specs,
        out_specs=pl.BlockSpec((seq, POOL_WIDTH), lambda b: (row0 // seq + b, 0)),
        input_output_aliases=aliases,
        compiler_params=_params(("parallel",)),
        name="pool",
    )(*args)


def _log_sigmoid(x):
    return jnp.minimum(x, 0.0) - jnp.log1p(jnp.exp(-jnp.abs(x)))


def _scan_cumsum(x, reverse, row_idx):
    n = x.shape[0]
    s = 1
    while s < n:
        if reverse:
            x = x + jnp.where(row_idx < n - s, pltpu.roll(x, n - s, 0), 0.0)
        else:
            x = x + jnp.where(row_idx >= s, pltpu.roll(x, s, 0), 0.0)
        s *= 2
    return x


def _mlstm_chunk(q, k, v, i_col, b_col, r_row, total, c_prev, n_prev, m_prev, vis):
    w = jnp.where(vis, r_row, -jnp.inf)
    inter = b_col + m_prev
    m_t = jnp.maximum(inter, b_col + jnp.max(w, axis=1, keepdims=True))
    w_inter = jnp.exp(inter - m_t)
    ks = k * (ML_HEAD_DIM ** -0.5)
    qb = q.astype(BF16)
    vb = v.astype(BF16)
    qk = _dot_nt(qb, ks.astype(BF16)) * jnp.exp(w + (b_col - m_t))
    num = (jnp.dot(qk.astype(BF16), vb, preferred_element_type=F32)
           + w_inter * jnp.dot(qb, c_prev.astype(BF16), preferred_element_type=F32))
    den = jnp.sum(qk, axis=1, keepdims=True) + w_inter * jnp.sum(q * n_prev, axis=1, keepdims=True)
    h = num * (1.0 / jnp.maximum(jnp.abs(den), jnp.exp(-m_t)))
    g = total - b_col + i_col
    m_new = jnp.maximum(total + m_prev, jnp.max(g, axis=0, keepdims=True))
    ws = jnp.exp(g - m_new)
    decay = jnp.exp(total + m_prev - m_new)
    kw = ks * ws
    c_new = decay * c_prev + lax.dot_general(kw.astype(BF16), vb, (((0,), (0,)), ((), ())),
                                             preferred_element_type=F32)
    n_new = decay * n_prev + jnp.sum(kw, axis=0, keepdims=True)
    return h, c_new, n_new, m_new


def _mlstm_kernel(*refs, has_init, n_prev, emit_state):
    qf, kf, vf, gf, qb, kb, vb, gb = refs[:8]
    pos = 8
    if has_init:
        c0_ref, n0_ref, m0_ref = refs[pos:pos + 3]
        pos += 3
    pos += n_prev
    hf_ref, hb_ref = refs[pos:pos + 2]
    pos += 2
    if emit_state:
        co_ref, no_ref, mo_ref = refs[pos:pos + 3]
        pos += 3
    c_s, n_s, m_s = refs[pos:pos + 3]
    c = pl.program_id(1)

    @pl.when(c == 0)
    def _():
        if has_init:
            c_s[...] = c0_ref[...]
            n_s[...] = n0_ref[...]
            m_s[...] = m0_ref[...]
        else:
            c_s[...] = jnp.zeros_like(c_s)
            n_s[...] = jnp.zeros_like(n_s)
            m_s[...] = jnp.zeros_like(m_s)

    t_idx = lax.broadcasted_iota(jnp.int32, (ML_CHUNK, ML_CHUNK), 0)
    s_idx = lax.broadcasted_iota(jnp.int32, (ML_CHUNK, ML_CHUNK), 1)
    row_idx = lax.broadcasted_iota(jnp.int32, (ML_CHUNK, GATE_PAD), 0)
    for d, (q_ref, k_ref, v_ref, g_ref, h_ref) in enumerate(((qf, kf, vf, gf, hf_ref), (qb, kb, vb, gb, hb_ref))):
        reverse = d == 1
        vis = (s_idx >= t_idx) if reverse else (s_idx <= t_idx)
        gates = g_ref[...]
        cum_f = _scan_cumsum(_log_sigmoid(gates), reverse, row_idx)
        total_row = cum_f[0:1, :] if reverse else cum_f[ML_CHUNK - 1:ML_CHUNK, :]
        gates_t = gates.T
        cum_f_t = cum_f.T
        for h in range(ML_HEADS):
            sl = slice(h * ML_HEAD_DIM, (h + 1) * ML_HEAD_DIM)
            ci = 2 * d * ML_HEADS + h
            cf = (2 * d + 1) * ML_HEADS + h
            out, c_new, n_new, m_new = _mlstm_chunk(
                q_ref[:, sl], k_ref[:, sl], v_ref[:, sl], gates[:, ci:ci + 1], cum_f[:, cf:cf + 1],
                gates_t[ci:ci + 1, :] - cum_f_t[cf:cf + 1, :], total_row[:, cf:cf + 1],
                c_s[d, h], n_s[d, h], m_s[d, h], vis)
            h_ref[:, sl] = out
            c_s[d, h] = c_new
            n_s[d, h] = n_new
            m_s[d, h] = m_new

    if emit_state:
        @pl.when(c == pl.num_programs(1) - 1)
        def _():
            co_ref[...] = c_s[...]
            no_ref[...] = n_s[...]
            mo_ref[...] = m_s[...]


def _mlstm(proj, gates, row0, n_seq, seq, layer, init=None, prev_h=None, emit_state=False, prev_state=None):
    nc = seq // ML_CHUNK
    dh = ML_HEAD_DIM
    rb = row0 // ML_CHUNK
    m_tot = proj.shape[0]

    def chunk_row(b, c, rev):
        return rb + b * nc + ((nc - 1 - c) if rev else c)

    def tok(col, rev):
        return pl.BlockSpec((ML_CHUNK, ML_WIDTH), lambda b, c: (chunk_row(b, c, rev), col // ML_WIDTH))

    def gat(rev):
        return pl.BlockSpec((ML_CHUNK, GATE_PAD), lambda b, c: (chunk_row(b, c, rev), 0))

    def hout(rev):
        return pl.BlockSpec((ML_CHUNK, ML_WIDTH), lambda b, c: (chunk_row(b, c, rev), 0))

    in_specs = [tok(COL_ML_Q, False), tok(COL_ML_K, False), tok(COL_ML_V, False), gat(False),
                tok(COL_ML_Q, True), tok(COL_ML_K, True), tok(COL_ML_V, True), gat(True)]
    args = [proj, proj, proj, gates, proj, proj, proj, gates]
    state_idx = lambda b, c: (b, layer, 0, 0, 0, 0)
    if init is not None:
        c0, n0, m0 = init
        in_specs += [_resident((None, None, 2, ML_HEADS, dh, dh), state_idx),
                     _resident((None, None, 2, ML_HEADS, 1, dh), state_idx),
                     _resident((None, None, 2, ML_HEADS, 1, 1), state_idx)]
        args += [c0, n0, m0]
    aliases = {}
    prevs = list(prev_h or ()) + list(prev_state or ())
    out_base = 0 if prev_h else 2
    for i, p in enumerate(prevs):
        aliases[len(args)] = out_base + i
        in_specs.append(_ANY)
        args.append(p)
    out_shape = [jax.ShapeDtypeStruct((m_tot, ML_WIDTH), F32)] * 2
    out_specs = [hout(False), hout(True)]
    if emit_state:
        out_shape += [jax.ShapeDtypeStruct((n_seq, DEPTH, 2, ML_HEADS, dh, dh), F32),
                      jax.ShapeDtypeStruct((n_seq, DEPTH, 2, ML_HEADS, 1, dh), F32),
                      jax.ShapeDtypeStruct((n_seq, DEPTH, 2, ML_HEADS, 1, 1), F32)]
        out_specs += [pl.BlockSpec((None, None, 2, ML_HEADS, dh, dh), state_idx),
                      pl.BlockSpec((None, None, 2, ML_HEADS, 1, dh), state_idx),
                      pl.BlockSpec((None, None, 2, ML_HEADS, 1, 1), state_idx)]
    return pl.pallas_call(
        functools.partial(_mlstm_kernel, has_init=init is not None, n_prev=len(prevs), emit_state=emit_state),
        out_shape=out_shape,
        grid=(n_seq, nc),
        in_specs=in_specs,
        out_specs=out_specs,
        input_output_aliases=aliases,
        scratch_shapes=[pltpu.VMEM((2, ML_HEADS, dh, dh), F32), pltpu.VMEM((2, ML_HEADS, 1, dh), F32),
                        pltpu.VMEM((2, ML_HEADS, 1, 1), F32)],
        compiler_params=_params(("parallel", "arbitrary")),
        name="mlstm",
    )(*args)


def _ml_post_kernel(hf_ref, hb_ref, o_ref, g_ref, out_ref):
    for h in range(ML_HEADS):
        sl = slice(h * ML_HEAD_DIM, (h + 1) * ML_HEAD_DIM)
        x = hf_ref[:, sl] + hb_ref[:, sl]
        x = x * lax.rsqrt(jnp.mean(x * x, axis=-1, keepdims=True) + EPS)
        x = x * g_ref[:, sl]
        out_ref[:, sl] = (jax.nn.sigmoid(o_ref[:, sl]) * x).astype(out_ref.dtype)


def _ml_post(hf, hb, proj, norm_g, bm=512):
    m = hf.shape[0]
    row = lambda i: (i, 0)
    return pl.pallas_call(
        _ml_post_kernel,
        out_shape=jax.ShapeDtypeStruct((m, ML_WIDTH), BF16),
        grid=(m // bm,),
        in_specs=[pl.BlockSpec((bm, ML_WIDTH), row), pl.BlockSpec((bm, ML_WIDTH), row),
                  pl.BlockSpec((bm, ML_WIDTH), lambda i: (i, COL_ML_O // ML_WIDTH)),
                  pl.BlockSpec((1, ML_WIDTH), lambda i: (0, 0))],
        out_specs=pl.BlockSpec((bm, ML_WIDTH), row),
        compiler_params=_params(("parallel",)),
        name="ml_post",
    )(hf, hb, proj, norm_g)


def kernel(x_prompt, x_sample, c, cache_na_k, cache_na_v, state_mlstm_C, state_mlstm_n, state_mlstm_m, c_ctx, w_ada, b_ada, norm_g, ffn_w_in, ffn_w_out, w_in, na_rel_bias, pool_w, pool_scale, ml_gate_bias, ml_norm_g, w_branch, w_out, final_norm_g):
    batch, seq, d = x_prompt.shape
    dec_batch, dec_seq, _ = x_sample.shape
    past = cache_na_k.shape[2]
    assert d == D_MODEL and dec_batch + 1 <= MOD_ROWS
    m_ctx = batch * seq
    m_lat = dec_batch * dec_seq
    groups = _Groups(m_ctx, dec_seq)

    m_tot = m_ctx + m_lat
    x = [(0, x_prompt.reshape(m_ctx, d)), (m_ctx, x_sample.reshape(m_lat, d))]
    cond =jnp.concatenate([c_ctx[None], c, jnp.zeros((MOD_ROWS - 1 - dec_batch, d), F32)], axis=0)
    mod_all = _modulation(cond, w_ada, b_ada.reshape(DEPTH, 1, N_MOD * d))
    mod_all = mod_all.reshape(DEPTH, MOD_ROWS, N_MOD, d)

    ctx_k = cache_na_k.reshape(dec_batch, DEPTH, past, NA_WIDTH)
    ctx_v = cache_na_v.reshape(dec_batch, DEPTH, past, NA_WIDTH)
    init = (state_mlstm_C,
            state_mlstm_n.reshape(dec_batch, DEPTH, 2, ML_HEADS, 1, ML_HEAD_DIM),
            state_mlstm_m.reshape(dec_batch, DEPTH, 2, ML_HEADS, 1, 1))
    w_in_t = jnp.swapaxes(w_in, 1, 2)

    kv_out = None
    states = None
    for l in range(DEPTH):
        mod = mod_all[l]
        b_gate = jnp.pad(ml_gate_bias[l], (0, GATE_PAD - GATE_COLS)).reshape(1, GATE_PAD)

        def ffn(x, j, shift_row):
            pieces = x if isinstance(x, list) else [(0, x)]
            g = norm_g[l, 2 * j].reshape(1, d)
            h = None
            for row0, xr in pieces:
                h = _norm_mod(xr, g, mod, groups, shift_row, row0=row0, out_rows=m_tot, prev=h)
            act, w_o = _mm_swiglu(h, ffn_w_in, ffn_w_out, l, j)
            y = None
            for row0, xr in pieces:
                y = _mm_resid(act, w_o, (), xr, mod, groups, shift_row + 2, MACARON_W, bm=1024, bn=256,
                              row0=row0, prev=y)
            return y

        x = ffn(x, 0, 0)

        h = _norm_mod(x, norm_g[l, 1].reshape(1, d), mod, groups, 3)
        proj, w_merge = _mm_proj(h, w_in_t, l)
        gates = _mm_gates(h, w_in_t, l, b_gate)

        na_out = _ctx_attention(proj, batch, seq)
        na_out = _na_attention(proj, m_ctx, dec_batch, dec_seq, ctx_k, ctx_v, l, _rel_bias_table(na_rel_bias[l]),
                               na_out)
        ps = pool_scale[l].reshape(1, POOL_WIDTH)
        pool_out = _pool(proj, 0, batch, seq, pool_w[l], ps)
        pool_out = _pool(proj, m_ctx, dec_batch, dec_seq, pool_w[l], ps, prev=pool_out)
        hf, hb, *states = _mlstm(proj, gates, 0, batch, seq, l, emit_state=True, prev_state=states)
        hf, hb = _mlstm(proj, gates, m_ctx, dec_batch, dec_seq, l, init=init, prev_h=(hf, hb))
        ml_out = _ml_post(hf, hb, proj, ml_norm_g[l].reshape(1, ML_WIDTH))

        merged = _mm_merge(h, na_out, pool_out, ml_out, w_merge, w_branch, l)
        x = _mm_resid(merged, w_out, (l,), x, mod, groups, 5, 1.0, bm=1024, bn=512)

        x = ffn(x, 1, 6)
        kv_out = _emit_kv(proj, batch, seq, l, prev=kv_out)

    g_fin = final_norm_g.reshape(1, d)
    y_prompt = _final_norm(x, 0, m_ctx, g_fin).reshape(batch, seq, d)
    y_sample = _final_norm(x, m_ctx, m_lat, g_fin).reshape(dec_batch, dec_seq, d)
    c_fin, n_fin, m_fin = states
    kv_shape = (batch, DEPTH, seq, NA_HEADS, NA_HEAD_DIM)
    return (y_prompt, y_sample, kv_out[0].reshape(kv_shape), kv_out[1].reshape(kv_shape),
            c_fin, n_fin.reshape(batch, DEPTH, 2, ML_HEADS, ML_HEAD_DIM), m_fin.reshape(batch, DEPTH, 2, ML_HEADS))
```

```python
import functools

import jax
import jax.numpy as jnp
import numpy as np
from jax import lax
from jax.experimental import pallas as pl
from jax.experimental.pallas import tpu as pltpu

F32 = jnp.float32
BF16 = jnp.bfloat16

D_MODEL = 4096
DEPTH = 2
GRID_W = 64
NA_HEADS = 8
NA_WIDTH = D_MODEL // 4
NA_HEAD_DIM = NA_WIDTH // NA_HEADS
NA_WIN_ROWS = 8
NA_WIN_COLS = 16
NA_ROWS_PER_STEP = 4
NA_SPAN_ROWS = NA_WIN_ROWS + NA_ROWS_PER_STEP
NA_MASKED_SLOT = 2 * NA_WIN_ROWS - 1
POOL_WINDOWS = (2, 4, 8, 16)
POOL_WIDTH = D_MODEL // 4
POOL_GROUP = POOL_WIDTH // 4
ML_HEADS = 8
ML_WIDTH = D_MODEL // 2
ML_HEAD_DIM = ML_WIDTH // ML_HEADS
ML_CHUNK = 256
D_FF = 256 * ((8 * D_MODEL // 3 + 255) // 256)
N_MOD = 9
MACARON_W = 0.5
EPS = 1e-6
MASK_VALUE = -1e30

COL_NA_Q = 0
COL_NA_K = NA_WIDTH
COL_NA_V = 2 * NA_WIDTH
COL_POOL = 3 * NA_WIDTH
COL_ML_Q = COL_POOL + POOL_WIDTH
COL_ML_K = COL_ML_Q + ML_WIDTH
COL_ML_V = COL_ML_K + ML_WIDTH
COL_ML_O = COL_ML_V + ML_WIDTH
MAIN_COLS = COL_ML_O + ML_WIDTH
GATE_COLS = 4 * ML_HEADS
GATE_PAD = 128

EPILOGUE_ROW_GROUPS = 2
MOD_ROWS = 8
VMEM_LIMIT = 56 * 1024 * 1024


def _params(sem, vmem=VMEM_LIMIT):
    return pltpu.CompilerParams(dimension_semantics=sem, vmem_limit_bytes=vmem)


def _resident(block_shape, index_map):
    return pl.BlockSpec(block_shape, index_map, pipeline_mode=pl.Buffered(1))


DOUBLE_BUFFER_MAX_BYTES = 12 * 1024 * 1024


def _outer_block(block_shape, index_map, itemsize):
    nbytes = itemsize * int(np.prod([d for d in block_shape if d is not None]))
    if nbytes > DOUBLE_BUFFER_MAX_BYTES:
        return _resident(block_shape, index_map)
    return pl.BlockSpec(block_shape, index_map)


_ANY = pl.BlockSpec(memory_space=pl.ANY)


def _modulation_kernel(c_ref, w_ref, b_ref, o_ref):
    c = c_ref[...]
    s = (c * jax.nn.sigmoid(c)).astype(BF16)
    o_ref[...] = jnp.dot(s, w_ref[...].astype(BF16), preferred_element_type=F32) + b_ref[...]


def _modulation(cond, w_ada, b_ada, bn=1024):
    n = w_ada.shape[-1]
    return pl.pallas_call(
        _modulation_kernel,
        out_shape=jax.ShapeDtypeStruct((DEPTH, MOD_ROWS, n), F32),
        grid=(DEPTH, n // bn),
        in_specs=[
            pl.BlockSpec((MOD_ROWS, D_MODEL), lambda l, j: (0, 0)),
            pl.BlockSpec((None, D_MODEL, bn), lambda l, j: (l, 0, j)),
            pl.BlockSpec((None, 1, bn), lambda l, j: (l, 0, j)),
        ],
        out_specs=pl.BlockSpec((None, MOD_ROWS, bn), lambda l, j: (l, 0, j)),
        compiler_params=_params(("parallel", "parallel")),
        name="modulation",
    )(cond, w_ada, b_ada)


def _norm_mod_kernel(x_ref, g_ref, mod_ref, *rest, shift_row):
    o_ref = rest[-1]
    x = x_ref[...]
    y = x * lax.rsqrt(jnp.mean(x * x, axis=-1, keepdims=True) + EPS)
    shift = mod_ref[shift_row:shift_row + 1, :]
    scale = mod_ref[shift_row + 1:shift_row + 2, :]
    o_ref[...] = (y * (g_ref[...] * (1 + scale)) + shift).astype(o_ref.dtype)


def _final_norm_kernel(x_ref, g_ref, o_ref):
    x = x_ref[...]
    y = x * lax.rsqrt(jnp.mean(x * x, axis=-1, keepdims=True) + EPS)
    o_ref[...] = y * g_ref[...]


class _Groups:
    def __init__(self, m_ctx, dec_seq):
        self.m_ctx = m_ctx
        self.dec_seq = dec_seq

    def of_block(self, i, bm):
        assert self.m_ctx % bm == 0 and self.dec_seq % bm == 0
        return jnp.maximum((i * bm) // self.dec_seq - (self.m_ctx // self.dec_seq - 1), 0)


def _norm_mod(x, g, mod, groups, shift_row, row0=0, out_rows=None, prev=None, bm=512):
    m, d = x.shape
    out_rows = m if out_rows is None else out_rows
    assert row0 % bm == 0
    b0 = row0 // bm
    in_specs = [pl.BlockSpec((bm, d), lambda i: (i, 0)),
                pl.BlockSpec((1, d), lambda i: (0, 0)),
                pl.BlockSpec((None, N_MOD, d), lambda i: (groups.of_block(b0 + i, bm), 0, 0))]
    args = [x, g, mod]
    aliases = {}
    if prev is not None:
        in_specs.append(_ANY)
        args.append(prev)
        aliases = {3: 0}
    return pl.pallas_call(
        functools.partial(_norm_mod_kernel, shift_row=shift_row),
        out_shape=jax.ShapeDtypeStruct((out_rows, d), BF16),
        grid=(m // bm,),
        in_specs=in_specs,
        out_specs=pl.BlockSpec((bm, d), lambda i: (b0 + i, 0)),
        input_output_aliases=aliases,
        compiler_params=_params(("parallel",)),
        name="norm_mod",
    )(*args)


def _final_norm(x, row0, rows, g, bm=512):
    d = x.shape[1]
    return pl.pallas_call(
        _final_norm_kernel,
        out_shape=jax.ShapeDtypeStruct((rows, d), F32),
        grid=(rows // bm,),
        in_specs=[pl.BlockSpec((bm, d), lambda i: (row0 // bm + i, 0)), pl.BlockSpec((1, d), lambda i: (0, 0))],
        out_specs=pl.BlockSpec((bm, d), lambda i: (i, 0)),
        compiler_params=_params(("parallel",)),
        name="final_norm",
    )(x, g)


def _dot_nt(a, b):
    return lax.dot_general(a, b, (((1,), (1,)), ((), ())), preferred_element_type=F32)


def _mm_proj_kernel(h_ref, wt_ref, *rest):
    *wg_parts, o_ref, wg_bf16_ref = rest
    part = wg_parts[0].shape[0]
    for p, wg_ref in enumerate(wg_parts):
        wg_bf16_ref[p * part:(p + 1) * part, :] = wg_ref[...].astype(BF16)
    o_ref[...] = _dot_nt(h_ref[...], wt_ref[...].astype(BF16))


def _mm_bias_kernel(h_ref, wt_ref, b_ref, o_ref):
    o_ref[...] = _dot_nt(h_ref[...], wt_ref[...].astype(BF16)) + b_ref[...]


def _mm_swiglu_kernel(h_ref, wa_ref, wb_ref, wo_ref, o_ref, wo_bf16_ref):
    wo_bf16_ref[...] = wo_ref[...].astype(BF16)
    wa = wa_ref[...].astype(BF16)
    wb = wb_ref[...].astype(BF16)
    half = h_ref.shape[0] // EPILOGUE_ROW_GROUPS
    for r in range(EPILOGUE_ROW_GROUPS):
        rows = slice(r * half, (r + 1) * half)
        h = h_ref[rows, :]
        a = jnp.dot(h, wa, preferred_element_type=F32)
        b = jnp.dot(h, wb, preferred_element_type=F32)
        o_ref[rows, :] = ((a * jax.nn.sigmoid(a)) * b).astype(o_ref.dtype)


def _mm_resid_kernel(a_ref, w_ref, x_ref, mod_ref, *rest, gate_row, coef):
    o_ref = rest[-1]
    y = jnp.dot(a_ref[...], w_ref[...].astype(BF16), preferred_element_type=F32)
    gate = mod_ref[gate_row:gate_row + 1, :]
    o_ref[...] = x_ref[...] + (coef * gate) * y


def _mm_merge_kernel(h_ref, na_ref, po_ref, ml_ref, wg_na, wg_po, wg_ml, wb_ref, o_ref):
    h = h_ref[...]

    def branch(x_ref, wg_ref, row0):
        wb = wb_ref[row0:row0 + x_ref.shape[1], :].astype(BF16)
        g = _dot_nt(h, wg_ref[...])
        y = jnp.dot(x_ref[...], wb, preferred_element_type=F32)
        return jax.nn.sigmoid(g) * y

    o = (branch(na_ref, wg_na, 0) + branch(po_ref, wg_po, NA_WIDTH)
         + branch(ml_ref, wg_ml, NA_WIDTH + POOL_WIDTH))
    o_ref[...] = o.astype(o_ref.dtype)


def _mm_proj(h, w_in_t, layer, bm=2048, bn=512):
    m, k = h.shape
    nb = MAIN_COLS // bn
    n_merge = w_in_t.shape[1] - MAIN_COLS - GATE_COLS
    slab = n_merge // ((m // bm) * nb)
    part = GATE_COLS
    parts = slab // part
    assert slab * (m // bm) * nb == n_merge and parts * part == slab and (MAIN_COLS + GATE_COLS) % part == 0
    first = (MAIN_COLS + GATE_COLS) // part

    def part_spec(p):
        return pl.BlockSpec((None, part, k), lambda i, j: (layer, first + parts * (i * nb + j) + p, 0))

    return pl.pallas_call(
        _mm_proj_kernel,
        out_shape=[jax.ShapeDtypeStruct((m, MAIN_COLS), F32), jax.ShapeDtypeStruct((n_merge, k), BF16)],
        grid=(m // bm, nb),
        in_specs=[_outer_block((bm, k), lambda i, j: (i, 0), h.dtype.itemsize),
                  pl.BlockSpec((None, bn, k), lambda i, j: (layer, j, 0))] + [part_spec(p) for p in range(parts)],
        out_specs=[pl.BlockSpec((bm, bn), lambda i, j: (i, j)),
                   pl.BlockSpec((slab, k), lambda i, j: (i * nb + j, 0))],
        compiler_params=_params(("parallel", "arbitrary")),
        name="mm_proj",
    )(h, w_in_t, *([w_in_t] * parts))


def _mm_gates(h, w_in_t, layer, b, bm=1024):
    m, k = h.shape
    assert MAIN_COLS % GATE_PAD == 0
    return pl.pallas_call(
        _mm_bias_kernel,
        out_shape=jax.ShapeDtypeStruct((m, GATE_PAD), F32),
        grid=(m // bm,),
        in_specs=[pl.BlockSpec((bm, k), lambda i: (i, 0)),
                  pl.BlockSpec((None, GATE_PAD, k), lambda i: (layer, MAIN_COLS // GATE_PAD, 0)),
                  pl.BlockSpec((1, GATE_PAD), lambda i: (0, 0))],
        out_specs=pl.BlockSpec((bm, GATE_PAD), lambda i: (i, 0)),
        compiler_params=_params(("parallel",)),
        name="mm_gates",
    )(h, w_in_t, b)


def _mm_swiglu(h, ffn_w_in, ffn_w_out, layer, j, bm=2048, bn=256):
    m, k = h.shape
    nb = D_FF // bn
    steps = (m // bm) * nb
    slab = D_FF // steps
    assert slab * steps == D_FF and slab % 16 == 0
    d_out = ffn_w_out.shape[-1]
    return pl.pallas_call(
        _mm_swiglu_kernel,
        out_shape=[jax.ShapeDtypeStruct((m, D_FF), BF16), jax.ShapeDtypeStruct((D_FF, d_out), BF16)],
        grid=(m // bm, nb),
        in_specs=[_resident((bm, k), lambda i, n: (i, 0)),
                  pl.BlockSpec((None, None, k, bn), lambda i, n: (layer, j, 0, n)),
                  pl.BlockSpec((None, None, k, bn), lambda i, n: (layer, j, 0, n + nb)),
                  pl.BlockSpec((None, None, slab, d_out), lambda i, n: (layer, j, i * nb + n, 0))],
        out_specs=[pl.BlockSpec((bm, bn), lambda i, n: (i, n)),
                   pl.BlockSpec((slab, d_out), lambda i, n: (i * nb + n, 0))],
        compiler_params=_params(("parallel", "arbitrary")),
        name="mm_swiglu",
    )(h, ffn_w_in, ffn_w_in, ffn_w_out)


def _mm_resid(a, w, w_index, x, mod, groups, gate_row, coef, bm, bn, row0=0, prev=None):
    m, k = a.shape
    n = w.shape[-1]
    rows = x.shape[0]
    assert row0 % bm == 0 and rows % bm == 0
    b0 = row0 // bm
    lead = (None,) * len(w_index)
    in_specs = [_outer_block((bm, k), lambda i, j: (b0 + i, 0), a.dtype.itemsize),
                pl.BlockSpec(lead + (k, bn), lambda i, j: w_index + (0, j)),
                pl.BlockSpec((bm, bn), lambda i, j: (i, j)),
                pl.BlockSpec((None, N_MOD, bn), lambda i, j: (groups.of_block(b0 + i, bm), 0, j))]
    args = [a, w, x, mod]
    aliases = {}
    if prev is not None:
        in_specs.append(_ANY)
        args.append(prev)
        aliases = {4: 0}
    return pl.pallas_call(
        functools.partial(_mm_resid_kernel, gate_row=gate_row, coef=coef),
        out_shape=jax.ShapeDtypeStruct((m, n), F32),
        grid=(rows // bm, n // bn),
        in_specs=in_specs,
        out_specs=pl.BlockSpec((bm, bn), lambda i, j: (b0 + i, j)),
        input_output_aliases=aliases,
        compiler_params=_params(("parallel", "arbitrary")),
        name="mm_resid",
    )(*args)


def _mm_merge(h, na, po, ml, wg, w_branch, layer, bm=1024, bn=256):
    m, d = h.shape
    nb = d // bn
    row = lambda i, j: (i, 0)
    return pl.pallas_call(
        _mm_merge_kernel,
        out_shape=jax.ShapeDtypeStruct((m, d), BF16),
        grid=(m // bm, nb),
        in_specs=[_resident((bm, d), row),
                  _resident((bm, NA_WIDTH), row),
                  _resident((bm, POOL_WIDTH), row),
                  _resident((bm, ML_WIDTH), row),
                  pl.BlockSpec((bn, d), lambda i, j: (j, 0)),
                  pl.BlockSpec((bn, d), lambda i, j: (j + nb, 0)),
                  pl.BlockSpec((bn, d), lambda i, j: (j + 2 * nb, 0)),
                  pl.BlockSpec((None, NA_WIDTH + POOL_WIDTH + ML_WIDTH, bn), lambda i, j: (layer, 0, j))],
        out_specs=pl.BlockSpec((bm, bn), lambda i, j: (i, j)),
        compiler_params=_params(("parallel", "arbitrary")),
        name="mm_merge",
    )(h, na, po, ml, wg, wg, wg, w_branch)


def _softmax_rows(parts):
    m = parts[0].max(axis=-1, keepdims=True)
    for s in parts[1:]:
        m = jnp.maximum(m, s.max(axis=-1, keepdims=True))
    es = [jnp.exp(s - m) for s in parts]
    den = es[0].sum(axis=-1, keepdims=True)
    for e in es[1:]:
        den = den + e.sum(axis=-1, keepdims=True)
    inv = 1.0 / den
    return [e * inv for e in es]


def _ctx_attn_kernel(q_ref, k_ref, v_ref, o_ref):
    scale = NA_HEAD_DIM ** -0.5
    for h in range(NA_HEADS):
        sl = slice(h * NA_HEAD_DIM, (h + 1) * NA_HEAD_DIM)
        q = q_ref[:, sl].astype(BF16)
        k = k_ref[:, sl].astype(BF16)
        v = v_ref[:, sl].astype(BF16)
        (p,) = _softmax_rows([_dot_nt(q, k) * scale])
        o_ref[:, sl] = jnp.dot(p.astype(BF16), v, preferred_element_type=F32).astype(o_ref.dtype)


def _ctx_attention(proj, n_seq, seq):
    blk = lambda c: pl.BlockSpec((seq, NA_WIDTH), lambda b: (b, c // NA_WIDTH))
    return pl.pallas_call(
        _ctx_attn_kernel,
        out_shape=jax.ShapeDtypeStruct((proj.shape[0], NA_WIDTH), BF16),
        grid=(n_seq,),
        in_specs=[blk(COL_NA_Q), blk(COL_NA_K), blk(COL_NA_V)],
        out_specs=pl.BlockSpec((seq, NA_WIDTH), lambda b: (b, 0)),
        compiler_params=_params(("parallel",)),
        name="ctx_attention",
    )(proj, proj, proj)


def _na_attn_kernel(q_ref, kf_ref, vf_ref, ckf_ref, cvf_ref, bias_ref, prev_ref, o_ref,
                    k_ref, v_ref, ck_ref, cv_ref, *, rows):
    del prev_ref
    r = pl.program_id(1)

    @pl.when(r == 0)
    def _():
        k_ref[...] = kf_ref[...].astype(BF16)
        v_ref[...] = vf_ref[...].astype(BF16)
        ck_ref[...] = ckf_ref[...].astype(BF16)
        cv_ref[...] = cvf_ref[...].astype(BF16)

    r0 = r * NA_ROWS_PER_STEP
    u = jnp.clip(r0 - NA_WIN_ROWS // 2, 0, rows - NA_SPAN_ROWS)
    k0 = pl.multiple_of(u * GRID_W, GRID_W)
    n_loc = NA_SPAN_ROWS * GRID_W
    slot = []
    for a in range(NA_ROWS_PER_STEP):
        start = jnp.clip(r0 + a - NA_WIN_ROWS // 2, 0, rows - NA_WIN_ROWS)
        slot.append([jnp.where((u + i >= start) & (u + i < start + NA_WIN_ROWS),
                               u + i - (r0 + a) + (NA_WIN_ROWS - 1), NA_MASKED_SLOT)
                     for i in range(NA_SPAN_ROWS)])
    scale = NA_HEAD_DIM ** -0.5
    for h in range(NA_HEADS):
        sl = slice(h * NA_HEAD_DIM, (h + 1) * NA_HEAD_DIM)
        q = q_ref[:, sl].astype(BF16)
        kl = k_ref[pl.ds(k0, n_loc), sl]
        vl = v_ref[pl.ds(k0, n_loc), sl]
        bias = jnp.concatenate([
            jnp.concatenate([bias_ref[h, 0, slot[a][2 * p]] + bias_ref[h, 1, slot[a][2 * p + 1]]
                             for p in range(NA_SPAN_ROWS // 2)], axis=-1)
            for a in range(NA_ROWS_PER_STEP)], axis=0)
        s_loc = jnp.where(bias > -jnp.inf, _dot_nt(q, kl) * scale + bias, MASK_VALUE)
        s_ctx = _dot_nt(q, ck_ref[:, sl]) * scale
        p_loc, p_ctx = _softmax_rows([s_loc, s_ctx])
        out = (jnp.dot(p_loc.astype(BF16), vl, preferred_element_type=F32)
               + jnp.dot(p_ctx.astype(BF16), cv_ref[:, sl], preferred_element_type=F32))
        o_ref[:, sl] = out.astype(o_ref.dtype)


def _na_attention(proj, row0, n_seq, seq, ctx_k, ctx_v, layer, bias_tab, prev):
    rows = seq // GRID_W
    assert rows % NA_ROWS_PER_STEP == 0 and rows >= NA_SPAN_ROWS
    steps = rows // NA_ROWS_PER_STEP
    bq = NA_ROWS_PER_STEP * GRID_W
    past = ctx_k.shape[2]
    rb = row0 // bq
    q_spec = pl.BlockSpec((bq, NA_WIDTH), lambda b, r: (rb + b * steps + r, COL_NA_Q // NA_WIDTH))
    kv = lambda c: _resident((seq, NA_WIDTH), lambda b, r: (row0 // seq + b, c // NA_WIDTH))
    ctx = _resident((None, None, past, NA_WIDTH), lambda b, r: (b, layer, 0, 0))
    return pl.pallas_call(
        functools.partial(_na_attn_kernel, rows=rows),
        out_shape=jax.ShapeDtypeStruct(prev.shape, prev.dtype),
        grid=(n_seq, steps),
        in_specs=[q_spec, kv(COL_NA_K), kv(COL_NA_V), ctx, ctx,
                  _resident(bias_tab.shape, lambda b, r: (0, 0, 0, 0, 0)), _ANY],
        out_specs=pl.BlockSpec((bq, NA_WIDTH), lambda b, r: (rb + b * steps + r, 0)),
        input_output_aliases={6: 0},
        scratch_shapes=[pltpu.VMEM((seq, NA_WIDTH), BF16), pltpu.VMEM((seq, NA_WIDTH), BF16),
                        pltpu.VMEM((past, NA_WIDTH), BF16), pltpu.VMEM((past, NA_WIDTH), BF16)],
        compiler_params=_params(("arbitrary", "arbitrary")),
        name="na_attention",
    )(proj, proj, proj, ctx_k, ctx_v, bias_tab, prev)


def _emit_kv_kernel(k_ref, v_ref, *rest):
    ko_ref, vo_ref = rest[-2:]
    ko_ref[...] = k_ref[...]
    vo_ref[...] = v_ref[...]


def _emit_kv(proj, n_seq, seq, layer, prev=None):
    blk = lambda c: pl.BlockSpec((seq, NA_WIDTH), lambda b: (b, c // NA_WIDTH))
    in_specs = [blk(COL_NA_K), blk(COL_NA_V)]
    args = [proj, proj]
    aliases = {}
    if prev is not None:
        in_specs += [_ANY, _ANY]
        args += list(prev)
        aliases = {2: 0, 3: 1}
    out = pl.BlockSpec((None, None, seq, NA_WIDTH), lambda b: (b, layer, 0, 0))
    return pl.pallas_call(
        _emit_kv_kernel,
        out_shape=[jax.ShapeDtypeStruct((n_seq, DEPTH, seq, NA_WIDTH), F32)] * 2,
        grid=(n_seq,),
        in_specs=in_specs,
        out_specs=[out, out],
        input_output_aliases=aliases,
        compiler_params=_params(("parallel",)),
        name="emit_kv",
    )(*args)


def _rel_bias_table(rel_bias):
    cq = np.arange(GRID_W)
    dc = np.clip(cq[None, :] - cq[:, None], -(NA_WIN_COLS - 1), NA_WIN_COLS - 1) + (NA_WIN_COLS - 1)
    col_start = np.clip(cq - NA_WIN_COLS // 2, 0, GRID_W - NA_WIN_COLS)
    col_mask = (cq[None, :] >= col_start[:, None]) & (cq[None, :] < col_start[:, None] + NA_WIN_COLS)
    tiles = jnp.where(col_mask, rel_bias[:, :, dc], -jnp.inf)
    tiles = jnp.concatenate([tiles, jnp.full((NA_HEADS, 1, GRID_W, GRID_W), -jnp.inf, F32)], axis=1)
    zeros = jnp.zeros_like(tiles)
    return jnp.stack([jnp.concatenate([tiles, zeros], axis=-1), jnp.concatenate([zeros, tiles], axis=-1)], axis=1)


def _pool_kernel(u_ref, w_ref, s_ref, *rest, seq):
    o_ref = rest[-1]
    pos = lax.broadcasted_iota(jnp.int32, (seq, POOL_GROUP), 0)
    for g, win in enumerate(POOL_WINDOWS):
        sl = slice(g * POOL_GROUP, (g + 1) * POOL_GROUP)
        u = u_ref[:, sl]
        acc = jnp.zeros_like(u)
        for d in range(-(win // 2), win - win // 2):
            shifted = u if d == 0 else pltpu.roll(u, (-d) % seq, 0)
            valid = (pos + d >= 0) & (pos + d < seq)
            acc = acc + jnp.where(valid, shifted, 0.0)
        lo = jnp.clip(pos - win // 2, 0, seq)
        hi = jnp.clip(pos + win - win // 2, 0, seq)
        pooled = acc / (hi - lo).astype(F32) - u
        y = jnp.dot(pooled.astype(BF16), w_ref[g].astype(BF16), preferred_element_type=F32)
        o_ref[:, sl] = (y * s_ref[:, sl]).astype(o_ref.dtype)


def _pool(proj, row0, n_seq, seq, pool_w, pool_scale, prev=None):
    in_specs = [pl.BlockSpec((seq, POOL_WIDTH), lambda b: (row0 // seq + b, COL_POOL // POOL_WIDTH)),
                pl.BlockSpec(pool_w.shape, lambda b: (0, 0, 0)),
                pl.BlockSpec((1, POOL_WIDTH), lambda b: (0, 0))]
    args = [proj, pool_w, pool_scale]
    aliases = {}
    if prev is not None:
        in_specs.append(_ANY)
        args.append(prev)
        aliases = {3: 0}
    return pl.pallas_call(
        functools.partial(_pool_kernel, seq=seq),
        out_shape=jax.ShapeDtypeStruct((proj.shape[0], POOL_WIDTH), BF16),
        grid=(n_seq,),
        in_specs=in_specs,
        out_specs=pl.BlockSpec((seq, POOL_WIDTH), lambda b: (row0 // seq + b, 0)),
        input_output_aliases=aliases,
        compiler_params=_params(("parallel",)),
        name="pool",
    )(*args)


def _log_sigmoid(x):
    return jnp.minimum(x, 0.0) - jnp.log1p(jnp.exp(-jnp.abs(x)))


def _scan_cumsum(x, reverse, row_idx):
    n = x.shape[0]
    s = 1
    while s < n:
        if reverse:
            x = x + jnp.where(row_idx < n - s, pltpu.roll(x, n - s, 0), 0.0)
        else:
            x = x + jnp.where(row_idx >= s, pltpu.roll(x, s, 0), 0.0)
        s *= 2
    return x


def _mlstm_chunk(q, k, v, i_col, b_col, r_row, total, c_prev, n_prev, m_prev, vis):
    w = jnp.where(vis, r_row, -jnp.inf)
    inter = b_col + m_prev
    m_t = jnp.maximum(inter, b_col + jnp.max(w, axis=1, keepdims=True))
    w_inter = jnp.exp(inter - m_t)
    ks = k * (ML_HEAD_DIM ** -0.5)
    qb = q.astype(BF16)
    vb = v.astype(BF16)
    qk = _dot_nt(qb, ks.astype(BF16)) * jnp.exp(w + (b_col - m_t))
    num = (jnp.dot(qk.astype(BF16), vb, preferred_element_type=F32)
           + w_inter * jnp.dot(qb, c_prev.astype(BF16), preferred_element_type=F32))
    den = jnp.sum(qk, axis=1, keepdims=True) + w_inter * jnp.sum(q * n_prev, axis=1, keepdims=True)
    h = num * (1.0 / jnp.maximum(jnp.abs(den), jnp.exp(-m_t)))
    g = total - b_col + i_col
    m_new = jnp.maximum(total + m_prev, jnp.max(g, axis=0, keepdims=True))
    ws = jnp.exp(g - m_new)
    decay = jnp.exp(total + m_prev - m_new)
    kw = ks * ws
    c_new = decay * c_prev + lax.dot_general(kw.astype(BF16), vb, (((0,), (0,)), ((), ())),
                                             preferred_element_type=F32)
    n_new = decay * n_prev + jnp.sum(kw, axis=0, keepdims=True)
    return h, c_new, n_new, m_new


def _mlstm_kernel(*refs, has_init, n_prev, emit_state):
    qf, kf, vf, gf, qb, kb, vb, gb = refs[:8]
    pos = 8
    if has_init:
        c0_ref, n0_ref, m0_ref = refs[pos:pos + 3]
        pos += 3
    pos += n_prev
    hf_ref, hb_ref = refs[pos:pos + 2]
    pos += 2
    if emit_state:
        co_ref, no_ref, mo_ref = refs[pos:pos + 3]
        pos += 3
    c_s, n_s, m_s = refs[pos:pos + 3]
    c = pl.program_id(1)

    @pl.when(c == 0)
    def _():
        if has_init:
            c_s[...] = c0_ref[...]
            n_s[...] = n0_ref[...]
            m_s[...] = m0_ref[...]
        else:
            c_s[...] = jnp.zeros_like(c_s)
            n_s[...] = jnp.zeros_like(n_s)
            m_s[...] = jnp.zeros_like(m_s)

    t_idx = lax.broadcasted_iota(jnp.int32, (ML_CHUNK, ML_CHUNK), 0)
    s_idx = lax.broadcasted_iota(jnp.int32, (ML_CHUNK, ML_CHUNK), 1)
    row_idx = lax.broadcasted_iota(jnp.int32, (ML_CHUNK, GATE_PAD), 0)
    for d, (q_ref, k_ref, v_ref, g_ref, h_ref) in enumerate(((qf, kf, vf, gf, hf_ref), (qb, kb, vb, gb, hb_ref))):
        reverse = d == 1
        vis = (s_idx >= t_idx) if reverse else (s_idx <= t_idx)
        gates = g_ref[...]
        cum_f = _scan_cumsum(_log_sigmoid(gates), reverse, row_idx)
        total_row = cum_f[0:1, :] if reverse else cum_f[ML_CHUNK - 1:ML_CHUNK, :]
        gates_t = gates.T
        cum_f_t = cum_f.T
        for h in range(ML_HEADS):
            sl = slice(h * ML_HEAD_DIM, (h + 1) * ML_HEAD_DIM)
            ci = 2 * d * ML_HEADS + h
            cf = (2 * d + 1) * ML_HEADS + h
            out, c_new, n_new, m_new = _mlstm_chunk(
                q_ref[:, sl], k_ref[:, sl], v_ref[:, sl], gates[:, ci:ci + 1], cum_f[:, cf:cf + 1],
                gates_t[ci:ci + 1, :] - cum_f_t[cf:cf + 1, :], total_row[:, cf:cf + 1],
                c_s[d, h], n_s[d, h], m_s[d, h], vis)
            h_ref[:, sl] = out
            c_s[d, h] = c_new
            n_s[d, h] = n_new
            m_s[d, h] = m_new

    if emit_state:
        @pl.when(c == pl.num_programs(1) - 1)
        def _():
            co_ref[...] = c_s[...]
            no_ref[...] = n_s[...]
            mo_ref[...] = m_s[...]


def _mlstm(proj, gates, row0, n_seq, seq, layer, init=None, prev_h=None, emit_state=False, prev_state=None):
    nc = seq // ML_CHUNK
    dh = ML_HEAD_DIM
    rb = row0 // ML_CHUNK
    m_tot = proj.shape[0]

    def chunk_row(b, c, rev):
        return rb + b * nc + ((nc - 1 - c) if rev else c)

    def tok(col, rev):
        return pl.BlockSpec((ML_CHUNK, ML_WIDTH), lambda b, c: (chunk_row(b, c, rev), col // ML_WIDTH))

    def gat(rev):
        return pl.BlockSpec((ML_CHUNK, GATE_PAD), lambda b, c: (chunk_row(b, c, rev), 0))

    def hout(rev):
        return pl.BlockSpec((ML_CHUNK, ML_WIDTH), lambda b, c: (chunk_row(b, c, rev), 0))

    in_specs = [tok(COL_ML_Q, False), tok(COL_ML_K, False), tok(COL_ML_V, False), gat(False),
                tok(COL_ML_Q, True), tok(COL_ML_K, True), tok(COL_ML_V, True), gat(True)]
    args = [proj, proj, proj, gates, proj, proj, proj, gates]
    state_idx = lambda b, c: (b, layer, 0, 0, 0, 0)
    if init is not None:
        c0, n0, m0 = init
        in_specs += [_resident((None, None, 2, ML_HEADS, dh, dh), state_idx),
                     _resident((None, None, 2, ML_HEADS, 1, dh), state_idx),
                     _resident((None, None, 2, ML_HEADS, 1, 1), state_idx)]
        args += [c0, n0, m0]
    aliases = {}
    prevs = list(prev_h or ()) + list(prev_state or ())
    out_base = 0 if prev_h else 2
    for i, p in enumerate(prevs):
        aliases[len(args)] = out_base + i
        in_specs.append(_ANY)
        args.append(p)
    out_shape = [jax.ShapeDtypeStruct((m_tot, ML_WIDTH), F32)] * 2
    out_specs = [hout(False), hout(True)]
    if emit_state:
        out_shape += [jax.ShapeDtypeStruct((n_seq, DEPTH, 2, ML_HEADS, dh, dh), F32),
                      jax.ShapeDtypeStruct((n_seq, DEPTH, 2, ML_HEADS, 1, dh), F32),
                      jax.ShapeDtypeStruct((n_seq, DEPTH, 2, ML_HEADS, 1, 1), F32)]
        out_specs += [pl.BlockSpec((None, None, 2, ML_HEADS, dh, dh), state_idx),
                      pl.BlockSpec((None, None, 2, ML_HEADS, 1, dh), state_idx),
                      pl.BlockSpec((None, None, 2, ML_HEADS, 1, 1), state_idx)]
    return pl.pallas_call(
        functools.partial(_mlstm_kernel, has_init=init is not None, n_prev=len(prevs), emit_state=emit_state),
        out_shape=out_shape,
        grid=(n_seq, nc),
        in_specs=in_specs,
        out_specs=out_specs,
        input_output_aliases=aliases,
        scratch_shapes=[pltpu.VMEM((2, ML_HEADS, dh, dh), F32), pltpu.VMEM((2, ML_HEADS, 1, dh), F32),
                        pltpu.VMEM((2, ML_HEADS, 1, 1), F32)],
        compiler_params=_params(("parallel", "arbitrary")),
        name="mlstm",
    )(*args)


def _ml_post_kernel(hf_ref, hb_ref, o_ref, g_ref, out_ref):
    for h in range(ML_HEADS):
        sl = slice(h * ML_HEAD_DIM, (h + 1) * ML_HEAD_DIM)
        x = hf_ref[:, sl] + hb_ref[:, sl]
        x = x * lax.rsqrt(jnp.mean(x * x, axis=-1, keepdims=True) + EPS)
        x = x * g_ref[:, sl]
        out_ref[:, sl] = (jax.nn.sigmoid(o_ref[:, sl]) * x).astype(out_ref.dtype)


def _ml_post(hf, hb, proj, norm_g, bm=512):
    m = hf.shape[0]
    row = lambda i: (i, 0)
    return pl.pallas_call(
        _ml_post_kernel,
        out_shape=jax.ShapeDtypeStruct((m, ML_WIDTH), BF16),
        grid=(m // bm,),
        in_specs=[pl.BlockSpec((bm, ML_WIDTH), row), pl.BlockSpec((bm, ML_WIDTH), row),
                  pl.BlockSpec((bm, ML_WIDTH), lambda i: (i, COL_ML_O // ML_WIDTH)),
                  pl.BlockSpec((1, ML_WIDTH), lambda i: (0, 0))],
        out_specs=pl.BlockSpec((bm, ML_WIDTH), row),
        compiler_params=_params(("parallel",)),
        name="ml_post",
    )(hf, hb, proj, norm_g)


def kernel(x_prompt, x_sample, c, cache_na_k, cache_na_v, state_mlstm_C, state_mlstm_n, state_mlstm_m, c_ctx, w_ada, b_ada, norm_g, ffn_w_in, ffn_w_out, w_in, na_rel_bias, pool_w, pool_scale, ml_gate_bias, ml_norm_g, w_branch, w_out, final_norm_g):
    batch, seq, d = x_prompt.shape
    dec_batch, dec_seq, _ = x_sample.shape
    past = cache_na_k.shape[2]
    assert d == D_MODEL and dec_batch + 1 <= MOD_ROWS
    m_ctx = batch * seq
    m_lat = dec_batch * dec_seq
    groups = _Groups(m_ctx, dec_seq)

    m_tot = m_ctx + m_lat
    x = [(0, x_prompt.reshape(m_ctx, d)), (m_ctx, x_sample.reshape(m_lat, d))]
    cond =jnp.concatenate([c_ctx[None], c, jnp.zeros((MOD_ROWS - 1 - dec_batch, d), F32)], axis=0)
    mod_all = _modulation(cond, w_ada, b_ada.reshape(DEPTH, 1, N_MOD * d))
    mod_all = mod_all.reshape(DEPTH, MOD_ROWS, N_MOD, d)

    ctx_k = cache_na_k.reshape(dec_batch, DEPTH, past, NA_WIDTH)
    ctx_v = cache_na_v.reshape(dec_batch, DEPTH, past, NA_WIDTH)
    init = (state_mlstm_C,
            state_mlstm_n.reshape(dec_batch, DEPTH, 2, ML_HEADS, 1, ML_HEAD_DIM),
            state_mlstm_m.reshape(dec_batch, DEPTH, 2, ML_HEADS, 1, 1))
    w_in_t = jnp.swapaxes(w_in, 1, 2)

    kv_out = None
    states = None
    for l in range(DEPTH):
        mod = mod_all[l]
        b_gate = jnp.pad(ml_gate_bias[l], (0, GATE_PAD - GATE_COLS)).reshape(1, GATE_PAD)

        def ffn(x, j, shift_row):
            pieces = x if isinstance(x, list) else [(0, x)]
            g = norm_g[l, 2 * j].reshape(1, d)
            h = None
            for row0, xr in pieces:
                h = _norm_mod(xr, g, mod, groups, shift_row, row0=row0, out_rows=m_tot, prev=h)
            act, w_o = _mm_swiglu(h, ffn_w_in, ffn_w_out, l, j)
            y = None
            for row0, xr in pieces:
                y = _mm_resid(act, w_o, (), xr, mod, groups, shift_row + 2, MACARON_W, bm=512, bn=512,
                              row0=row0, prev=y)
            return y

        x = ffn(x, 0, 0)

        h = _norm_mod(x, norm_g[l, 1].reshape(1, d), mod, groups, 3)
        proj, w_merge = _mm_proj(h, w_in_t, l)
        gates = _mm_gates(h, w_in_t, l, b_gate)

        na_out = _ctx_attention(proj, batch, seq)
        na_out = _na_attention(proj, m_ctx, dec_batch, dec_seq, ctx_k, ctx_v, l, _rel_bias_table(na_rel_bias[l]),
                               na_out)
        ps = pool_scale[l].reshape(1, POOL_WIDTH)
        pool_out = _pool(proj, 0, batch, seq, pool_w[l], ps)
        pool_out = _pool(proj, m_ctx, dec_batch, dec_seq, pool_w[l], ps, prev=pool_out)
        hf, hb, *states = _mlstm(proj, gates, 0, batch, seq, l, emit_state=True, prev_state=states)
        hf, hb = _mlstm(proj, gates, m_ctx, dec_batch, dec_seq, l, init=init, prev_h=(hf, hb))
        ml_out = _ml_post(hf, hb, proj, ml_norm_g[l].reshape(1, ML_WIDTH))

        merged = _mm_merge(h, na_out, pool_out, ml_out, w_merge, w_branch, l)
        x = _mm_resid(merged, w_out, (l,), x, mod, groups, 5, 1.0, bm=1024, bn=512)

        x = ffn(x, 1, 6)
        kv_out = _emit_kv(proj, batch, seq, l, prev=kv_out)

    g_fin = final_norm_g.reshape(1, d)
    y_prompt = _final_norm(x, 0, m_ctx, g_fin).reshape(batch, seq, d)
    y_sample = _final_norm(x, m_ctx, m_lat, g_fin).reshape(dec_batch, dec_seq, d)
    c_fin, n_fin, m_fin = states
    kv_shape = (batch, DEPTH, seq, NA_HEADS, NA_HEAD_DIM)
    return (y_prompt, y_sample, kv_out[0].reshape(kv_shape), kv_out[1].reshape(kv_shape),
            c_fin, n_fin.reshape(batch, DEPTH, 2, ML_HEADS, ML_HEAD_DIM), m_fin.reshape(batch, DEPTH, 2, ML_HEADS))
```

```python
import functools

import jax
import jax.numpy as jnp
import numpy as np
from jax import lax
from jax.experimental import pallas as pl
from jax.experimental.pallas import tpu as pltpu

F32 = jnp.float32
BF16 = jnp.bfloat16

D_MODEL = 4096
DEPTH = 2
GRID_W = 64
NA_HEADS = 8
NA_WIDTH = D_MODEL // 4
NA_HEAD_DIM = NA_WIDTH // NA_HEADS
NA_WIN_ROWS = 8
NA_WIN_COLS = 16
NA_ROWS_PER_STEP = 4
NA_SPAN_ROWS = NA_WIN_ROWS + NA_ROWS_PER_STEP
NA_MASKED_SLOT = 2 * NA_WIN_ROWS - 1
POOL_WINDOWS = (2, 4, 8, 16)
POOL_WIDTH = D_MODEL // 4
POOL_GROUP = POOL_WIDTH // 4
ML_HEADS = 8
ML_WIDTH = D_MODEL // 2
ML_HEAD_DIM = ML_WIDTH // ML_HEADS
ML_CHUNK = 256
D_FF = 256 * ((8 * D_MODEL // 3 + 255) // 256)
N_MOD = 9
MACARON_W = 0.5
EPS = 1e-6
MASK_VALUE = -1e30

COL_NA_Q = 0
COL_NA_K = NA_WIDTH
COL_NA_V = 2 * NA_WIDTH
COL_POOL = 3 * NA_WIDTH
COL_ML_Q = COL_POOL + POOL_WIDTH
COL_ML_K = COL_ML_Q + ML_WIDTH
COL_ML_V = COL_ML_K + ML_WIDTH
COL_ML_O = COL_ML_V + ML_WIDTH
MAIN_COLS = COL_ML_O + ML_WIDTH
GATE_COLS = 4 * ML_HEADS
GATE_PAD = 128

EPILOGUE_ROW_GROUPS = 2
MOD_ROWS = 8
VMEM_LIMIT = 56 * 1024 * 1024


def _params(sem, vmem=VMEM_LIMIT):
    return pltpu.CompilerParams(dimension_semantics=sem, vmem_limit_bytes=vmem)


def _resident(block_shape, index_map):
    return pl.BlockSpec(block_shape, index_map, pipeline_mode=pl.Buffered(1))


DOUBLE_BUFFER_MAX_BYTES = 12 * 1024 * 1024


def _outer_block(block_shape, index_map, itemsize):
    nbytes = itemsize * int(np.prod([d for d in block_shape if d is not None]))
    if nbytes > DOUBLE_BUFFER_MAX_BYTES:
        return _resident(block_shape, index_map)
    return pl.BlockSpec(block_shape, index_map)


_ANY = pl.BlockSpec(memory_space=pl.ANY)


def _modulation_kernel(c_ref, w_ref, b_ref, o_ref):
    c = c_ref[...]
    s = (c * jax.nn.sigmoid(c)).astype(BF16)
    o_ref[...] = jnp.dot(s, w_ref[...].astype(BF16), preferred_element_type=F32) + b_ref[...]


def _modulation(cond, w_ada, b_ada, bn=1024):
    n = w_ada.shape[-1]
    return pl.pallas_call(
        _modulation_kernel,
        out_shape=jax.ShapeDtypeStruct((DEPTH, MOD_ROWS, n), F32),
        grid=(DEPTH, n // bn),
        in_specs=[
            pl.BlockSpec((MOD_ROWS, D_MODEL), lambda l, j: (0, 0)),
            pl.BlockSpec((None, D_MODEL, bn), lambda l, j: (l, 0, j)),
            pl.BlockSpec((None, 1, bn), lambda l, j: (l, 0, j)),
        ],
        out_specs=pl.BlockSpec((None, MOD_ROWS, bn), lambda l, j: (l, 0, j)),
        compiler_params=_params(("parallel", "parallel")),
        name="modulation",
    )(cond, w_ada, b_ada)


def _norm_mod_kernel(x_ref, g_ref, mod_ref, *rest, shift_row):
    o_ref = rest[-1]
    x = x_ref[...]
    y = x * lax.rsqrt(jnp.mean(x * x, axis=-1, keepdims=True) + EPS)
    shift = mod_ref[shift_row:shift_row + 1, :]
    scale = mod_ref[shift_row + 1:shift_row + 2, :]
    o_ref[...] = (y * (g_ref[...] * (1 + scale)) + shift).astype(o_ref.dtype)


def _final_norm_kernel(x_ref, g_ref, o_ref):
    x = x_ref[...]
    y = x * lax.rsqrt(jnp.mean(x * x, axis=-1, keepdims=True) + EPS)
    o_ref[...] = y * g_ref[...]


class _Groups:
    def __init__(self, m_ctx, dec_seq):
        self.m_ctx = m_ctx
        self.dec_seq = dec_seq

    def of_block(self, i, bm):
        assert self.m_ctx % bm == 0 and self.dec_seq % bm == 0
        return jnp.maximum((i * bm) // self.dec_seq - (self.m_ctx // self.dec_seq - 1), 0)


def _norm_mod(x, g, mod, groups, shift_row, row0=0, out_rows=None, prev=None, bm=512):
    m, d = x.shape
    out_rows = m if out_rows is None else out_rows
    assert row0 % bm == 0
    b0 = row0 // bm
    in_specs = [pl.BlockSpec((bm, d), lambda i: (i, 0)),
                pl.BlockSpec((1, d), lambda i: (0, 0)),
                pl.BlockSpec((None, N_MOD, d), lambda i: (groups.of_block(b0 + i, bm), 0, 0))]
    args = [x, g, mod]
    aliases = {}
    if prev is not None:
        in_specs.append(_ANY)
        args.append(prev)
        aliases = {3: 0}
    return pl.pallas_call(
        functools.partial(_norm_mod_kernel, shift_row=shift_row),
        out_shape=jax.ShapeDtypeStruct((out_rows, d), BF16),
        grid=(m // bm,),
        in_specs=in_specs,
        out_specs=pl.BlockSpec((bm, d), lambda i: (b0 + i, 0)),
        input_output_aliases=aliases,
        compiler_params=_params(("parallel",)),
        name="norm_mod",
    )(*args)


def _final_norm(x, row0, rows, g, bm=512):
    d = x.shape[1]
    return pl.pallas_call(
        _final_norm_kernel,
        out_shape=jax.ShapeDtypeStruct((rows, d), F32),
        grid=(rows // bm,),
        in_specs=[pl.BlockSpec((bm, d), lambda i: (row0 // bm + i, 0)), pl.BlockSpec((1, d), lambda i: (0, 0))],
        out_specs=pl.BlockSpec((bm, d), lambda i: (i, 0)),
        compiler_params=_params(("parallel",)),
        name="final_norm",
    )(x, g)


def _dot_nt(a, b):
    return lax.dot_general(a, b, (((1,), (1,)), ((), ())), preferred_element_type=F32)


def _mm_proj_kernel(h_ref, wt_ref, *rest):
    *wg_parts, o_ref, wg_bf16_ref = rest
    part = wg_parts[0].shape[0]
    for p, wg_ref in enumerate(wg_parts):
        wg_bf16_ref[p * part:(p + 1) * part, :] = wg_ref[...].astype(BF16)
    o_ref[...] = _dot_nt(h_ref[...], wt_ref[...].astype(BF16))


def _mm_bias_kernel(h_ref, wt_ref, b_ref, o_ref):
    o_ref[...] = _dot_nt(h_ref[...], wt_ref[...].astype(BF16)) + b_ref[...]


def _mm_swiglu_kernel(h_ref, wa_ref, wb_ref, wo_ref, o_ref, wo_bf16_ref):
    wo_bf16_ref[...] = wo_ref[...].astype(BF16)
    wa = wa_ref[...].astype(BF16)
    wb = wb_ref[...].astype(BF16)
    half = h_ref.shape[0] // EPILOGUE_ROW_GROUPS
    for r in range(EPILOGUE_ROW_GROUPS):
        rows = slice(r * half, (r + 1) * half)
        h = h_ref[rows, :]
        a = jnp.dot(h, wa, preferred_element_type=F32)
        b = jnp.dot(h, wb, preferred_element_type=F32)
        o_ref[rows, :] = ((a * jax.nn.sigmoid(a)) * b).astype(o_ref.dtype)


def _mm_resid_kernel(a_ref, w_ref, x_ref, mod_ref, *rest, gate_row, coef):
    o_ref = rest[-1]
    y = jnp.dot(a_ref[...], w_ref[...].astype(BF16), preferred_element_type=F32)
    gate = mod_ref[gate_row:gate_row + 1, :]
    o_ref[...] = x_ref[...] + (coef * gate) * y


def _mm_merge_kernel(h_ref, na_ref, po_ref, ml_ref, wg_na, wg_po, wg_ml, wb_ref, o_ref):
    h = h_ref[...]

    def branch(x_ref, wg_ref, row0):
        wb = wb_ref[row0:row0 + x_ref.shape[1], :].astype(BF16)
        g = _dot_nt(h, wg_ref[...])
        y = jnp.dot(x_ref[...], wb, preferred_element_type=F32)
        return jax.nn.sigmoid(g) * y

    o = (branch(na_ref, wg_na, 0) + branch(po_ref, wg_po, NA_WIDTH)
         + branch(ml_ref, wg_ml, NA_WIDTH + POOL_WIDTH))
    o_ref[...] = o.astype(o_ref.dtype)


def _mm_proj(h, w_in_t, layer, bm=2048, bn=512):
    m, k = h.shape
    nb = MAIN_COLS // bn
    n_merge = w_in_t.shape[1] - MAIN_COLS - GATE_COLS
    slab = n_merge // ((m // bm) * nb)
    part = GATE_COLS
    parts = slab // part
    assert slab * (m // bm) * nb == n_merge and parts * part == slab and (MAIN_COLS + GATE_COLS) % part == 0
    first = (MAIN_COLS + GATE_COLS) // part

    def part_spec(p):
        return pl.BlockSpec((None, part, k), lambda i, j: (layer, first + parts * (i * nb + j) + p, 0))

    return pl.pallas_call(
        _mm_proj_kernel,
        out_shape=[jax.ShapeDtypeStruct((m, MAIN_COLS), F32), jax.ShapeDtypeStruct((n_merge, k), BF16)],
        grid=(m // bm, nb),
        in_specs=[_outer_block((bm, k), lambda i, j: (i, 0), h.dtype.itemsize),
                  pl.BlockSpec((None, bn, k), lambda i, j: (layer, j, 0))] + [part_spec(p) for p in range(parts)],
        out_specs=[pl.BlockSpec((bm, bn), lambda i, j: (i, j)),
                   pl.BlockSpec((slab, k), lambda i, j: (i * nb + j, 0))],
        compiler_params=_params(("parallel", "arbitrary")),
        name="mm_proj",
    )(h, w_in_t, *([w_in_t] * parts))


def _mm_gates(h, w_in_t, layer, b, bm=1024):
    m, k = h.shape
    assert MAIN_COLS % GATE_PAD == 0
    return pl.pallas_call(
        _mm_bias_kernel,
        out_shape=jax.ShapeDtypeStruct((m, GATE_PAD), F32),
        grid=(m // bm,),
        in_specs=[pl.BlockSpec((bm, k), lambda i: (i, 0)),
                  pl.BlockSpec((None, GATE_PAD, k), lambda i: (layer, MAIN_COLS // GATE_PAD, 0)),
                  pl.BlockSpec((1, GATE_PAD), lambda i: (0, 0))],
        out_specs=pl.BlockSpec((bm, GATE_PAD), lambda i: (i, 0)),
        compiler_params=_params(("parallel",)),
        name="mm_gates",
    )(h, w_in_t, b)


def _mm_swiglu(h, ffn_w_in, ffn_w_out, layer, j, bm=2048, bn=256):
    m, k = h.shape
    nb = D_FF // bn
    steps = (m // bm) * nb
    slab = D_FF // steps
    assert slab * steps == D_FF and slab % 16 == 0
    d_out = ffn_w_out.shape[-1]
    return pl.pallas_call(
        _mm_swiglu_kernel,
        out_shape=[jax.ShapeDtypeStruct((m, D_FF), BF16), jax.ShapeDtypeStruct((D_FF, d_out), BF16)],
        grid=(m // bm, nb),
        in_specs=[_resident((bm, k), lambda i, n: (i, 0)),
                  pl.BlockSpec((None, None, k, bn), lambda i, n: (layer, j, 0, n)),
                  pl.BlockSpec((None, None, k, bn), lambda i, n: (layer, j, 0, n + nb)),
                  pl.BlockSpec((None, None, slab, d_out), lambda i, n: (layer, j, i * nb + n, 0))],
        out_specs=[pl.BlockSpec((bm, bn), lambda i, n: (i, n)),
                   pl.BlockSpec((slab, d_out), lambda i, n: (i * nb + n, 0))],
        compiler_params=_params(("parallel", "arbitrary")),
        name="mm_swiglu",
    )(h, ffn_w_in, ffn_w_in, ffn_w_out)


def _mm_resid(a, w, w_index, x, mod, groups, gate_row, coef, bm, bn, row0=0, prev=None):
    m, k = a.shape
    n = w.shape[-1]
    rows = x.shape[0]
    assert row0 % bm == 0 and rows % bm == 0
    b0 = row0 // bm
    lead = (None,) * len(w_index)
    in_specs = [_outer_block((bm, k), lambda i, j: (b0 + i, 0), a.dtype.itemsize),
                pl.BlockSpec(lead + (k, bn), lambda i, j: w_index + (0, j)),
                pl.BlockSpec((bm, bn), lambda i, j: (i, j)),
                pl.BlockSpec((None, N_MOD, bn), lambda i, j: (groups.of_block(b0 + i, bm), 0, j))]
    args = [a, w, x, mod]
    aliases = {}
    if prev is not None:
        in_specs.append(_ANY)
        args.append(prev)
        aliases = {4: 0}
    return pl.pallas_call(
        functools.partial(_mm_resid_kernel, gate_row=gate_row, coef=coef),
        out_shape=jax.ShapeDtypeStruct((m, n), F32),
        grid=(rows // bm, n // bn),
        in_specs=in_specs,
        out_specs=pl.BlockSpec((bm, bn), lambda i, j: (b0 + i, j)),
        input_output_aliases=aliases,
        compiler_params=_params(("parallel", "arbitrary")),
        name="mm_resid",
    )(*args)


def _mm_merge(h, na, po, ml, wg, w_branch, layer, bm=1024, bn=256):
    m, d = h.shape
    nb = d // bn
    row = lambda i, j: (i, 0)
    return pl.pallas_call(
        _mm_merge_kernel,
        out_shape=jax.ShapeDtypeStruct((m, d), BF16),
        grid=(m // bm, nb),
        in_specs=[_resident((bm, d), row),
                  _resident((bm, NA_WIDTH), row),
                  _resident((bm, POOL_WIDTH), row),
                  _resident((bm, ML_WIDTH), row),
                  pl.BlockSpec((bn, d), lambda i, j: (j, 0)),
                  pl.BlockSpec((bn, d), lambda i, j: (j + nb, 0)),
                  pl.BlockSpec((bn, d), lambda i, j: (j + 2 * nb, 0)),
                  pl.BlockSpec((None, NA_WIDTH + POOL_WIDTH + ML_WIDTH, bn), lambda i, j: (layer, 0, j))],
        out_specs=pl.BlockSpec((bm, bn), lambda i, j: (i, j)),
        compiler_params=_params(("parallel", "arbitrary")),
        name="mm_merge",
    )(h, na, po, ml, wg, wg, wg, w_branch)


def _softmax_rows(parts):
    m = parts[0].max(axis=-1, keepdims=True)
    for s in parts[1:]:
        m = jnp.maximum(m, s.max(axis=-1, keepdims=True))
    es = [jnp.exp(s - m) for s in parts]
    den = es[0].sum(axis=-1, keepdims=True)
    for e in es[1:]:
        den = den + e.sum(axis=-1, keepdims=True)
    inv = 1.0 / den
    return [e * inv for e in es]


def _ctx_attn_kernel(q_ref, k_ref, v_ref, o_ref):
    scale = NA_HEAD_DIM ** -0.5
    for h in range(NA_HEADS):
        sl = slice(h * NA_HEAD_DIM, (h + 1) * NA_HEAD_DIM)
        q = q_ref[:, sl].astype(BF16)
        k = k_ref[:, sl].astype(BF16)
        v = v_ref[:, sl].astype(BF16)
        (p,) = _softmax_rows([_dot_nt(q, k) * scale])
        o_ref[:, sl] = jnp.dot(p.astype(BF16), v, preferred_element_type=F32).astype(o_ref.dtype)


def _ctx_attention(proj, n_seq, seq):
    blk = lambda c: pl.BlockSpec((seq, NA_WIDTH), lambda b: (b, c // NA_WIDTH))
    return pl.pallas_call(
        _ctx_attn_kernel,
        out_shape=jax.ShapeDtypeStruct((proj.shape[0], NA_WIDTH), BF16),
        grid=(n_seq,),
        in_specs=[blk(COL_NA_Q), blk(COL_NA_K), blk(COL_NA_V)],
        out_specs=pl.BlockSpec((seq, NA_WIDTH), lambda b: (b, 0)),
        compiler_params=_params(("parallel",)),
        name="ctx_attention",
    )(proj, proj, proj)


def _na_attn_kernel(q_ref, kf_ref, vf_ref, ckf_ref, cvf_ref, bias_ref, prev_ref, o_ref,
                    k_ref, v_ref, ck_ref, cv_ref, *, rows):
    del prev_ref
    r = pl.program_id(1)

    @pl.when(r == 0)
    def _():
        k_ref[...] = kf_ref[...].astype(BF16)
        v_ref[...] = vf_ref[...].astype(BF16)
        ck_ref[...] = ckf_ref[...].astype(BF16)
        cv_ref[...] = cvf_ref[...].astype(BF16)

    r0 = r * NA_ROWS_PER_STEP
    u = jnp.clip(r0 - NA_WIN_ROWS // 2, 0, rows - NA_SPAN_ROWS)
    k0 = pl.multiple_of(u * GRID_W, GRID_W)
    n_loc = NA_SPAN_ROWS * GRID_W
    slot = []
    for a in range(NA_ROWS_PER_STEP):
        start = jnp.clip(r0 + a - NA_WIN_ROWS // 2, 0, rows - NA_WIN_ROWS)
        slot.append([jnp.where((u + i >= start) & (u + i < start + NA_WIN_ROWS),
                               u + i - (r0 + a) + (NA_WIN_ROWS - 1), NA_MASKED_SLOT)
                     for i in range(NA_SPAN_ROWS)])
    scale = NA_HEAD_DIM ** -0.5
    for h in range(NA_HEADS):
        sl = slice(h * NA_HEAD_DIM, (h + 1) * NA_HEAD_DIM)
        q = q_ref[:, sl].astype(BF16)
        kl = k_ref[pl.ds(k0, n_loc), sl]
        vl = v_ref[pl.ds(k0, n_loc), sl]
        bias = jnp.concatenate([
            jnp.concatenate([bias_ref[h, 0, slot[a][2 * p]] + bias_ref[h, 1, slot[a][2 * p + 1]]
                             for p in range(NA_SPAN_ROWS // 2)], axis=-1)
            for a in range(NA_ROWS_PER_STEP)], axis=0)
        s_loc = jnp.where(bias > -jnp.inf, _dot_nt(q, kl) * scale + bias, MASK_VALUE)
        s_ctx = _dot_nt(q, ck_ref[:, sl]) * scale
        p_loc, p_ctx = _softmax_rows([s_loc, s_ctx])
        out = (jnp.dot(p_loc.astype(BF16), vl, preferred_element_type=F32)
               + jnp.dot(p_ctx.astype(BF16), cv_ref[:, sl], preferred_element_type=F32))
        o_ref[:, sl] = out.astype(o_ref.dtype)


def _na_attention(proj, row0, n_seq, seq, ctx_k, ctx_v, layer, bias_tab, prev):
    rows = seq // GRID_W
    assert rows % NA_ROWS_PER_STEP == 0 and rows >= NA_SPAN_ROWS
    steps = rows // NA_ROWS_PER_STEP
    bq = NA_ROWS_PER_STEP * GRID_W
    past = ctx_k.shape[2]
    rb = row0 // bq
    q_spec = pl.BlockSpec((bq, NA_WIDTH), lambda b, r: (rb + b * steps + r, COL_NA_Q // NA_WIDTH))
    kv = lambda c: _resident((seq, NA_WIDTH), lambda b, r: (row0 // seq + b, c // NA_WIDTH))
    ctx = _resident((None, None, past, NA_WIDTH), lambda b, r: (b, layer, 0, 0))
    return pl.pallas_call(
        functools.partial(_na_attn_kernel, rows=rows),
        out_shape=jax.ShapeDtypeStruct(prev.shape, prev.dtype),
        grid=(n_seq, steps),
        in_specs=[q_spec, kv(COL_NA_K), kv(COL_NA_V), ctx, ctx,
                  _resident((NA_HEADS,) + bias_tab.shape[1:], lambda b, r: (layer, 0, 0, 0, 0)), _ANY],
        out_specs=pl.BlockSpec((bq, NA_WIDTH), lambda b, r: (rb + b * steps + r, 0)),
        input_output_aliases={6: 0},
        scratch_shapes=[pltpu.VMEM((seq, NA_WIDTH), BF16), pltpu.VMEM((seq, NA_WIDTH), BF16),
                        pltpu.VMEM((past, NA_WIDTH), BF16), pltpu.VMEM((past, NA_WIDTH), BF16)],
        compiler_params=_params(("arbitrary", "arbitrary")),
        name="na_attention",
    )(proj, proj, proj, ctx_k, ctx_v, bias_tab, prev)


def _emit_kv_kernel(k_ref, v_ref, *rest):
    ko_ref, vo_ref = rest[-2:]
    ko_ref[...] = k_ref[...]
    vo_ref[...] = v_ref[...]


def _emit_kv(proj, n_seq, seq, layer, prev=None):
    blk = lambda c: pl.BlockSpec((seq, NA_WIDTH), lambda b: (b, c // NA_WIDTH))
    in_specs = [blk(COL_NA_K), blk(COL_NA_V)]
    args = [proj, proj]
    aliases = {}
    if prev is not None:
        in_specs += [_ANY, _ANY]
        args += list(prev)
        aliases = {2: 0, 3: 1}
    out = pl.BlockSpec((None, None, seq, NA_WIDTH), lambda b: (b, layer, 0, 0))
    return pl.pallas_call(
        _emit_kv_kernel,
        out_shape=[jax.ShapeDtypeStruct((n_seq, DEPTH, seq, NA_WIDTH), F32)] * 2,
        grid=(n_seq,),
        in_specs=in_specs,
        out_specs=[out, out],
        input_output_aliases=aliases,
        compiler_params=_params(("parallel",)),
        name="emit_kv",
    )(*args)


def _rel_bias_table(rel_bias):
    cq = np.arange(GRID_W)
    dc = np.clip(cq[None, :] - cq[:, None], -(NA_WIN_COLS - 1), NA_WIN_COLS - 1) + (NA_WIN_COLS - 1)
    col_start = np.clip(cq - NA_WIN_COLS // 2, 0, GRID_W - NA_WIN_COLS)
    col_mask = (cq[None, :] >= col_start[:, None]) & (cq[None, :] < col_start[:, None] + NA_WIN_COLS)
    tiles = jnp.where(col_mask, rel_bias[:, :, dc], -jnp.inf)
    tiles = jnp.concatenate([tiles, jnp.full((rel_bias.shape[0], 1, GRID_W, GRID_W), -jnp.inf, F32)], axis=1)
    zeros = jnp.zeros_like(tiles)
    return jnp.stack([jnp.concatenate([tiles, zeros], axis=-1), jnp.concatenate([zeros, tiles], axis=-1)], axis=1)


def _pool_kernel(u_ref, w_ref, s_ref, *rest, seq):
    o_ref = rest[-1]
    pos = lax.broadcasted_iota(jnp.int32, (seq, POOL_GROUP), 0)
    for g, win in enumerate(POOL_WINDOWS):
        sl = slice(g * POOL_GROUP, (g + 1) * POOL_GROUP)
        u = u_ref[:, sl]
        acc = jnp.zeros_like(u)
        for d in range(-(win // 2), win - win // 2):
            shifted = u if d == 0 else pltpu.roll(u, (-d) % seq, 0)
            valid = (pos + d >= 0) & (pos + d < seq)
            acc = acc + jnp.where(valid, shifted, 0.0)
        lo = jnp.clip(pos - win // 2, 0, seq)
        hi = jnp.clip(pos + win - win // 2, 0, seq)
        pooled = acc / (hi - lo).astype(F32) - u
        y = jnp.dot(pooled.astype(BF16), w_ref[g].astype(BF16), preferred_element_type=F32)
        o_ref[:, sl] = (y * s_ref[:, sl]).astype(o_ref.dtype)


def _pool(proj, row0, n_seq, seq, pool_w, pool_scale, prev=None):
    in_specs = [pl.BlockSpec((seq, POOL_WIDTH), lambda b: (row0 // seq + b, COL_POOL // POOL_WIDTH)),
                pl.BlockSpec(pool_w.shape, lambda b: (0, 0, 0)),
                pl.BlockSpec((1, POOL_WIDTH), lambda b: (0, 0))]
    args = [proj, pool_w, pool_scale]
    aliases = {}
    if prev is not None:
        in_specs.append(_ANY)
        args.append(prev)
        aliases = {3: 0}
    return pl.pallas_call(
        functools.partial(_pool_kernel, seq=seq),
        out_shape=jax.ShapeDtypeStruct((proj.shape[0], POOL_WIDTH), BF16),
        grid=(n_seq,),
        in_specs=in_specs,
        out_specs=pl.BlockSpec((seq, POOL_WIDTH), lambda b: (row0 // seq + b, 0)),
        input_output_aliases=aliases,
        compiler_params=_params(("parallel",)),
        name="pool",
    )(*args)


def _log_sigmoid(x):
    return jnp.minimum(x, 0.0) - jnp.log1p(jnp.exp(-jnp.abs(x)))


def _scan_cumsum(x, reverse, row_idx):
    n = x.shape[0]
    s = 1
    while s < n:
        if reverse:
            x = x + jnp.where(row_idx < n - s, pltpu.roll(x, n - s, 0), 0.0)
        else:
            x = x + jnp.where(row_idx >= s, pltpu.roll(x, s, 0), 0.0)
        s *= 2
    return x


def _mlstm_chunk(q, k, v, i_col, b_col, r_row, total, c_prev, n_prev, m_prev, vis):
    w = jnp.where(vis, r_row, -jnp.inf)
    inter = b_col + m_prev
    m_t = jnp.maximum(inter, b_col + jnp.max(w, axis=1, keepdims=True))
    w_inter = jnp.exp(inter - m_t)
    ks = k * (ML_HEAD_DIM ** -0.5)
    qb = q.astype(BF16)
    vb = v.astype(BF16)
    qk = _dot_nt(qb, ks.astype(BF16)) * jnp.exp(w + (b_col - m_t))
    num = (jnp.dot(qk.astype(BF16), vb, preferred_element_type=F32)
           + w_inter * jnp.dot(qb, c_prev.astype(BF16), preferred_element_type=F32))
    den = jnp.sum(qk, axis=1, keepdims=True) + w_inter * jnp.sum(q * n_prev, axis=1, keepdims=True)
    h = num * (1.0 / jnp.maximum(jnp.abs(den), jnp.exp(-m_t)))
    g = total - b_col + i_col
    m_new = jnp.maximum(total + m_prev, jnp.max(g, axis=0, keepdims=True))
    ws = jnp.exp(g - m_new)
    decay = jnp.exp(total + m_prev - m_new)
    kw = ks * ws
    c_new = decay * c_prev + lax.dot_general(kw.astype(BF16), vb, (((0,), (0,)), ((), ())),
                                             preferred_element_type=F32)
    n_new = decay * n_prev + jnp.sum(kw, axis=0, keepdims=True)
    return h, c_new, n_new, m_new


def _mlstm_kernel(*refs, has_init, n_prev, emit_state):
    qf, kf, vf, gf, qb, kb, vb, gb = refs[:8]
    pos = 8
    if has_init:
        c0_ref, n0_ref, m0_ref = refs[pos:pos + 3]
        pos += 3
    pos += n_prev
    hf_ref, hb_ref = refs[pos:pos + 2]
    pos += 2
    if emit_state:
        co_ref, no_ref, mo_ref = refs[pos:pos + 3]
        pos += 3
    c_s, n_s, m_s = refs[pos:pos + 3]
    c = pl.program_id(1)

    @pl.when(c == 0)
    def _():
        if has_init:
            c_s[...] = c0_ref[...]
            n_s[...] = n0_ref[...]
            m_s[...] = m0_ref[...]
        else:
            c_s[...] = jnp.zeros_like(c_s)
            n_s[...] = jnp.zeros_like(n_s)
            m_s[...] = jnp.zeros_like(m_s)

    t_idx = lax.broadcasted_iota(jnp.int32, (ML_CHUNK, ML_CHUNK), 0)
    s_idx = lax.broadcasted_iota(jnp.int32, (ML_CHUNK, ML_CHUNK), 1)
    row_idx = lax.broadcasted_iota(jnp.int32, (ML_CHUNK, GATE_PAD), 0)
    for d, (q_ref, k_ref, v_ref, g_ref, h_ref) in enumerate(((qf, kf, vf, gf, hf_ref), (qb, kb, vb, gb, hb_ref))):
        reverse = d == 1
        vis = (s_idx >= t_idx) if reverse else (s_idx <= t_idx)
        gates = g_ref[...]
        cum_f = _scan_cumsum(_log_sigmoid(gates), reverse, row_idx)
        total_row = cum_f[0:1, :] if reverse else cum_f[ML_CHUNK - 1:ML_CHUNK, :]
        gates_t = gates.T
        cum_f_t = cum_f.T
        for h in range(ML_HEADS):
            sl = slice(h * ML_HEAD_DIM, (h + 1) * ML_HEAD_DIM)
            ci = 2 * d * ML_HEADS + h
            cf = (2 * d + 1) * ML_HEADS + h
            out, c_new, n_new, m_new = _mlstm_chunk(
                q_ref[:, sl], k_ref[:, sl], v_ref[:, sl], gates[:, ci:ci + 1], cum_f[:, cf:cf + 1],
                gates_t[ci:ci + 1, :] - cum_f_t[cf:cf + 1, :], total_row[:, cf:cf + 1],
                c_s[d, h], n_s[d, h], m_s[d, h], vis)
            h_ref[:, sl] = out
            c_s[d, h] = c_new
            n_s[d, h] = n_new
            m_s[d, h] = m_new

    if emit_state:
        @pl.when(c == pl.num_programs(1) - 1)
        def _():
            co_ref[...] = c_s[...]
            no_ref[...] = n_s[...]
            mo_ref[...] = m_s[...]


def _mlstm(proj, gates, row0, n_seq, seq, layer, init=None, prev_h=None, emit_state=False, prev_state=None):
    nc = seq // ML_CHUNK
    dh = ML_HEAD_DIM
    rb = row0 // ML_CHUNK
    m_tot = proj.shape[0]

    def chunk_row(b, c, rev):
        return rb + b * nc + ((nc - 1 - c) if rev else c)

    def tok(col, rev):
        return pl.BlockSpec((ML_CHUNK, ML_WIDTH), lambda b, c: (chunk_row(b, c, rev), col // ML_WIDTH))

    def gat(rev):
        return pl.BlockSpec((ML_CHUNK, GATE_PAD), lambda b, c: (chunk_row(b, c, rev), 0))

    def hout(rev):
        return pl.BlockSpec((ML_CHUNK, ML_WIDTH), lambda b, c: (chunk_row(b, c, rev), 0))

    in_specs = [tok(COL_ML_Q, False), tok(COL_ML_K, False), tok(COL_ML_V, False), gat(False),
                tok(COL_ML_Q, True), tok(COL_ML_K, True), tok(COL_ML_V, True), gat(True)]
    args = [proj, proj, proj, gates, proj, proj, proj, gates]
    state_idx = lambda b, c: (b, layer, 0, 0, 0, 0)
    if init is not None:
        c0, n0, m0 = init
        in_specs += [_resident((None, None, 2, ML_HEADS, dh, dh), state_idx),
                     _resident((None, None, 2, ML_HEADS, 1, dh), state_idx),
                     _resident((None, None, 2, ML_HEADS, 1, 1), state_idx)]
        args += [c0, n0, m0]
    aliases = {}
    prevs = list(prev_h or ()) + list(prev_state or ())
    out_base = 0 if prev_h else 2
    for i, p in enumerate(prevs):
        aliases[len(args)] = out_base + i
        in_specs.append(_ANY)
        args.append(p)
    out_shape = [jax.ShapeDtypeStruct((m_tot, ML_WIDTH), F32)] * 2
    out_specs = [hout(False), hout(True)]
    if emit_state:
        out_shape += [jax.ShapeDtypeStruct((n_seq, DEPTH, 2, ML_HEADS, dh, dh), F32),
                      jax.ShapeDtypeStruct((n_seq, DEPTH, 2, ML_HEADS, 1, dh), F32),
                      jax.ShapeDtypeStruct((n_seq, DEPTH, 2, ML_HEADS, 1, 1), F32)]
        out_specs += [pl.BlockSpec((None, None, 2, ML_HEADS, dh, dh), state_idx),
                      pl.BlockSpec((None, None, 2, ML_HEADS, 1, dh), state_idx),
                      pl.BlockSpec((None, None, 2, ML_HEADS, 1, 1), state_idx)]
    return pl.pallas_call(
        functools.partial(_mlstm_kernel, has_init=init is not None, n_prev=len(prevs), emit_state=emit_state),
        out_shape=out_shape,
        grid=(n_seq, nc),
        in_specs=in_specs,
        out_specs=out_specs,
        input_output_aliases=aliases,
        scratch_shapes=[pltpu.VMEM((2, ML_HEADS, dh, dh), F32), pltpu.VMEM((2, ML_HEADS, 1, dh), F32),
                        pltpu.VMEM((2, ML_HEADS, 1, 1), F32)],
        compiler_params=_params(("parallel", "arbitrary")),
        name="mlstm",
    )(*args)


def _ml_post_kernel(hf_ref, hb_ref, o_ref, g_ref, out_ref):
    for h in range(ML_HEADS):
        sl = slice(h * ML_HEAD_DIM, (h + 1) * ML_HEAD_DIM)
        x = hf_ref[:, sl] + hb_ref[:, sl]
        x = x * lax.rsqrt(jnp.mean(x * x, axis=-1, keepdims=True) + EPS)
        x = x * g_ref[:, sl]
        out_ref[:, sl] = (jax.nn.sigmoid(o_ref[:, sl]) * x).astype(out_ref.dtype)


def _ml_post(hf, hb, proj, norm_g, bm=512):
    m = hf.shape[0]
    row = lambda i: (i, 0)
    return pl.pallas_call(
        _ml_post_kernel,
        out_shape=jax.ShapeDtypeStruct((m, ML_WIDTH), BF16),
        grid=(m // bm,),
        in_specs=[pl.BlockSpec((bm, ML_WIDTH), row), pl.BlockSpec((bm, ML_WIDTH), row),
                  pl.BlockSpec((bm, ML_WIDTH), lambda i: (i, COL_ML_O // ML_WIDTH)),
                  pl.BlockSpec((1, ML_WIDTH), lambda i: (0, 0))],
        out_specs=pl.BlockSpec((bm, ML_WIDTH), row),
        compiler_params=_params(("parallel",)),
        name="ml_post",
    )(hf, hb, proj, norm_g)


def kernel(x_prompt, x_sample, c, cache_na_k, cache_na_v, state_mlstm_C, state_mlstm_n, state_mlstm_m, c_ctx, w_ada, b_ada, norm_g, ffn_w_in, ffn_w_out, w_in, na_rel_bias, pool_w, pool_scale, ml_gate_bias, ml_norm_g, w_branch, w_out, final_norm_g):
    batch, seq, d = x_prompt.shape
    dec_batch, dec_seq, _ = x_sample.shape
    past = cache_na_k.shape[2]
    assert d == D_MODEL and dec_batch + 1 <= MOD_ROWS
    m_ctx = batch * seq
    m_lat = dec_batch * dec_seq
    groups = _Groups(m_ctx, dec_seq)

    m_tot = m_ctx + m_lat
    x = [(0, x_prompt.reshape(m_ctx, d)), (m_ctx, x_sample.reshape(m_lat, d))]
    cond =jnp.concatenate([c_ctx[None], c, jnp.zeros((MOD_ROWS - 1 - dec_batch, d), F32)], axis=0)
    mod_all = _modulation(cond, w_ada, b_ada.reshape(DEPTH, 1, N_MOD * d))
    mod_all = mod_all.reshape(DEPTH, MOD_ROWS, N_MOD, d)

    ctx_k = cache_na_k.reshape(dec_batch, DEPTH, past, NA_WIDTH)
    ctx_v = cache_na_v.reshape(dec_batch, DEPTH, past, NA_WIDTH)
    init = (state_mlstm_C,
            state_mlstm_n.reshape(dec_batch, DEPTH, 2, ML_HEADS, 1, ML_HEAD_DIM),
            state_mlstm_m.reshape(dec_batch, DEPTH, 2, ML_HEADS, 1, 1))
    w_in_t = jnp.swapaxes(w_in, 1, 2)
    bias_tab = _rel_bias_table(na_rel_bias.reshape((DEPTH * NA_HEADS,) + na_rel_bias.shape[2:]))

    kv_out = None
    states = None
    for l in range(DEPTH):
        mod = mod_all[l]
        b_gate = jnp.pad(ml_gate_bias[l], (0, GATE_PAD - GATE_COLS)).reshape(1, GATE_PAD)

        def ffn(x, j, shift_row):
            pieces = x if isinstance(x, list) else [(0, x)]
            g = norm_g[l, 2 * j].reshape(1, d)
            h = None
            for row0, xr in pieces:
                h = _norm_mod(xr, g, mod, groups, shift_row, row0=row0, out_rows=m_tot, prev=h)
            act, w_o = _mm_swiglu(h, ffn_w_in, ffn_w_out, l, j)
            y = None
            for row0, xr in pieces:
                y = _mm_resid(act, w_o, (), xr, mod, groups, shift_row + 2, MACARON_W, bm=512, bn=512,
                              row0=row0, prev=y)
            return y

        x = ffn(x, 0, 0)

        h = _norm_mod(x, norm_g[l, 1].reshape(1, d), mod, groups, 3)
        proj, w_merge = _mm_proj(h, w_in_t, l)
        gates = _mm_gates(h, w_in_t, l, b_gate)

        na_out = _ctx_attention(proj, batch, seq)
        na_out = _na_attention(proj, m_ctx, dec_batch, dec_seq, ctx_k, ctx_v, l, bias_tab, na_out)
        ps = pool_scale[l].reshape(1, POOL_WIDTH)
        pool_out = _pool(proj, 0, batch, seq, pool_w[l], ps)
        pool_out = _pool(proj, m_ctx, dec_batch, dec_seq, pool_w[l], ps, prev=pool_out)
        hf, hb, *states = _mlstm(proj, gates, 0, batch, seq, l, emit_state=True, prev_state=states)
        hf, hb = _mlstm(proj, gates, m_ctx, dec_batch, dec_seq, l, init=init, prev_h=(hf, hb))
        ml_out = _ml_post(hf, hb, proj, ml_norm_g[l].reshape(1, ML_WIDTH))

        merged = _mm_merge(h, na_out, pool_out, ml_out, w_merge, w_branch, l)
        x = _mm_resid(merged, w_out, (l,), x, mod, groups, 5, 1.0, bm=2048, bn=512)

        x = ffn(x, 1, 6)
        kv_out = _emit_kv(proj, batch, seq, l, prev=kv_out)

    g_fin = final_norm_g.reshape(1, d)
    y_prompt = _final_norm(x, 0, m_ctx, g_fin).reshape(batch, seq, d)
    y_sample = _final_norm(x, m_ctx, m_lat, g_fin).reshape(dec_batch, dec_seq, d)
    c_fin, n_fin, m_fin = states
    kv_shape = (batch, DEPTH, seq, NA_HEADS, NA_HEAD_DIM)
    return (y_prompt, y_sample, kv_out[0].reshape(kv_shape), kv_out[1].reshape(kv_shape),
            c_fin, n_fin.reshape(batch, DEPTH, 2, ML_HEADS, ML_HEAD_DIM), m_fin.reshape(batch, DEPTH, 2, ML_HEADS))
```
